```python
import jax, jax.numpy as jnp
from jax import lax
import numpy as np

D_MODEL = 1024
BATCH = 8
SEQ = 2048
DEPTH = 2

CHUNK = 64
EPS = 1e-6
GLA_HEADS = 4
GLA_DK = D_MODEL // 2
GLA_DV = D_MODEL
GLA_HK = GLA_DK // GLA_HEADS
GLA_HV = GLA_DV // GLA_HEADS
GLA_GATE_RANK = 16
GLA_GATE_NORMALIZER = 16.0
SSM_EXPAND = 2
SSM_DINNER = SSM_EXPAND * D_MODEL
SSM_HEADDIM = 64
SSM_HEADS = SSM_DINNER // SSM_HEADDIM
SSM_GROUPS = 8
SSM_HPG = SSM_HEADS // SSM_GROUPS
SSM_DSTATE = 128
SSM_CONV = 4
SSM_CONV_DIM = SSM_DINNER + 2 * SSM_GROUPS * SSM_DSTATE
MOE_GROUPS = 4
MOE_EPG = 8
MOE_EXPERTS = MOE_GROUPS * MOE_EPG
MOE_TOPK = 2
MOE_DFF = 256
N_BRANCH = 2
N_MOD = 6
IN_SIZES = (GLA_DK, GLA_DK, GLA_DV, GLA_DV, GLA_GATE_RANK, SSM_DINNER, SSM_CONV_DIM, SSM_HEADS, N_BRANCH * D_MODEL)
D_IN = GLA_DK + GLA_DK + GLA_DV + GLA_DV + GLA_GATE_RANK + SSM_DINNER + SSM_CONV_DIM + SSM_HEADS + N_BRANCH * D_MODEL

kernel_name = 'hybrid_gla_ssd_hmoe_adaln'


def split_cols(t, sizes):
    out, start = [], 0
    for s in sizes:
        out.append(t[..., start:start + s])
        start += s
    return out


def rms_norm(x, g):
    xf = x.astype(jnp.float32)
    y = xf * lax.rsqrt(jnp.mean(xf * xf, axis=-1, keepdims=True) + EPS)
    return y.astype(x.dtype) * g


def causal_depthwise_conv(u, w, b):
    out = lax.conv_general_dilated(u, w[:, None, :].astype(u.dtype), window_strides=(1,),
                                   padding=[(SSM_CONV - 1, 0)],
                                   dimension_numbers=('NWC', 'WIO', 'NWC'),
                                   feature_group_count=u.shape[-1])
    return out + b


def chunk_recurrence(decay, update):
    def step(s, inp):
        a, u = inp
        return a * s + u, s
    s0 = jnp.zeros_like(update[0])
    _, prev = lax.scan(step, s0, (decay, update))
    return prev


def gla_mixer(q, k, v, g, gk_low, w_gk2, b_gk, norm_g):
    bsz, s = q.shape[0], q.shape[1]
    nc = s // CHUNK
    gk = jax.nn.log_sigmoid((gk_low @ w_gk2 + b_gk).astype(jnp.float32)) / GLA_GATE_NORMALIZER
    q = q.reshape(bsz, nc, CHUNK, GLA_HEADS, GLA_HK) * (GLA_HK ** -0.5)
    k = k.reshape(bsz, nc, CHUNK, GLA_HEADS, GLA_HK)
    v = v.reshape(bsz, nc, CHUNK, GLA_HEADS, GLA_HV)
    gk = gk.reshape(bsz, nc, CHUNK, GLA_HEADS, GLA_HK)
    g_cum = jnp.cumsum(gk, axis=2)
    g_end = g_cum[:, :, -1]
    k_dec = k * jnp.exp(g_end[:, :, None] - g_cum)
    update = jnp.einsum('bclhk,bclhv->cbhkv', k_dec, v)
    decay = jnp.exp(g_end).transpose(1, 0, 2, 3)[..., None]
    prev = chunk_recurrence(decay, update)
    state = decay * prev + update
    o = jnp.einsum('bclhk,cbhkv->bclhv', q, state)
    o = rms_norm(o, norm_g)
    return o.reshape(bsz, s, GLA_DV) * jax.nn.silu(g)


def ssd_mixer(z, xbc, dt_raw, conv_w, conv_b, dt_bias, a_log, d_skip, norm_g):
    bsz, s = z.shape[0], z.shape[1]
    nc = s // CHUNK
    xbc = jax.nn.silu(causal_depthwise_conv(xbc, conv_w, conv_b))
    xs, b_m, c_m = split_cols(xbc, (SSM_DINNER, SSM_GROUPS * SSM_DSTATE, SSM_GROUPS * SSM_DSTATE))
    dt = jax.nn.softplus(dt_raw.astype(jnp.float32) + dt_bias)
    a = -jnp.exp(a_log.astype(jnp.float32))
    d_a = (dt * a).reshape(bsz, nc, CHUNK, SSM_GROUPS, SSM_HPG)
    dt = dt.reshape(bsz, nc, CHUNK, SSM_GROUPS, SSM_HPG)
    xh = xs.reshape(bsz, nc, CHUNK, SSM_GROUPS, SSM_HPG, SSM_HEADDIM)
    b_m = b_m.reshape(bsz, nc, CHUNK, SSM_GROUPS, SSM_DSTATE)
    c_m = c_m.reshape(bsz, nc, CHUNK, SSM_GROUPS, SSM_DSTATE)
    a_cum = jnp.cumsum(d_a, axis=2)
    xdt = xh * dt[..., None]
    seg = a_cum[:, :, :, None] - a_cum[:, :, None, :]
    l_mask = jnp.exp(-jnp.abs(seg))
    cb = jnp.einsum('bctgn,bcsgn->bctsg', c_m, b_m)
    y_diag = jnp.einsum('bctsg,bctsgr,bcsgrp->bctgrp', cb, l_mask, xdt)
    dec_end = jnp.exp(a_cum[:, :, -1:] - a_cum)
    update = jnp.einsum('bclgn,bclgr,bclgrp->cbgrpn', b_m, dec_end, xdt)
    chunk_decay = jnp.exp(a_cum[:, :, -1]).transpose(1, 0, 2, 3)[..., None, None]
    prev = chunk_recurrence(chunk_decay, update)
    y_off = jnp.einsum('bctgn,cbgrpn,bctgr->bctgrp', c_m, prev, jnp.exp(a_cum))
    y = y_diag + y_off + xh * d_skip.reshape(SSM_GROUPS, SSM_HPG)[..., None]
    y = y.reshape(bsz, s, SSM_DINNER) * jax.nn.silu(z)
    gsz = SSM_DINNER // SSM_GROUPS
    y = rms_norm(y.reshape(bsz, s, SSM_GROUPS, gsz), norm_g.reshape(SSM_GROUPS, gsz))
    return y.reshape(bsz, s, SSM_DINNER)


def hier_moe(h, w_rg, b_rg, w_re, b_re, w1, w3, w2):
    bsz, s, d = h.shape
    t = bsz * s
    hf = h.reshape(t, d)
    group_prob = jax.nn.softmax((hf @ w_rg + b_rg).astype(jnp.float32), axis=-1)
    g_w, g_idx = lax.top_k(group_prob, 1)
    expert_logits = (hf @ w_re + b_re).astype(jnp.float32).reshape(t, MOE_GROUPS, MOE_EPG)
    sel_logits = jnp.einsum('tg,tge->te', jax.nn.one_hot(g_idx[:, 0], MOE_GROUPS, dtype=jnp.float32), expert_logits)
    e_prob = jax.nn.softmax(sel_logits, axis=-1)
    e_w, e_idx = lax.top_k(e_prob, MOE_TOPK)
    e_w = e_w / jnp.sum(e_w, axis=-1, keepdims=True)
    weights = g_w * e_w
    global_idx = g_idx * MOE_EPG + e_idx
    combine = jnp.sum(jax.nn.one_hot(global_idx, MOE_EXPERTS, dtype=jnp.float32) * weights[..., None], axis=1)
    combine = combine.reshape(t, MOE_GROUPS, MOE_EPG).astype(h.dtype)
    out = jnp.zeros((t, d), dtype=h.dtype)
    for gi in range(MOE_GROUPS):
        hid = jax.nn.silu(jnp.einsum('td,edf->tef', hf, w1[gi])) * jnp.einsum('td,edf->tef', hf, w3[gi])
        out = out + jnp.einsum('tef,efd->td', hid * combine[:, gi, :, None], w2[gi])
    return out.reshape(bsz, s, d)


def setup_inputs(seed: int = 0) -> dict:
    key = jax.random.key(seed)
    ks = jax.random.split(key, 32)
    f32 = jnp.float32
    nrm = lambda k, shape, scale: jax.random.normal(k, shape, f32) * scale
    dt0 = jnp.exp(jax.random.uniform(ks[10], (DEPTH, SSM_HEADS), f32, np.log(1e-3), np.log(1e-1)))
    return {
        'x': nrm(ks[0], (BATCH, SEQ, D_MODEL), 1.0),
        'c': nrm(ks[1], (BATCH, D_MODEL), 1.0),
        'ada_w': nrm(ks[2], (DEPTH, D_MODEL, N_MOD * D_MODEL), D_MODEL ** -0.5),
        'ada_b': nrm(ks[3], (DEPTH, N_MOD * D_MODEL), 0.02),
        'norm1_g': 1.0 + nrm(ks[4], (DEPTH, D_MODEL), 0.02),
        'w_in': nrm(ks[5], (DEPTH, D_MODEL, D_IN), D_MODEL ** -0.5),
        'gla_w_gk2': nrm(ks[6], (DEPTH, GLA_GATE_RANK, GLA_DK), GLA_GATE_RANK ** -0.5),
        'gla_b_gk': nrm(ks[7], (DEPTH, GLA_DK), 0.1),
        'gla_norm_g': 1.0 + nrm(ks[8], (DEPTH, GLA_HV), 0.02),
        'conv_w': nrm(ks[9], (DEPTH, SSM_CONV, SSM_CONV_DIM), SSM_CONV ** -0.5),
        'conv_b': nrm(ks[11], (DEPTH, SSM_CONV_DIM), 0.02),
        'dt_bias': dt0 + jnp.log(-jnp.expm1(-dt0)),
        'a_log': jnp.log(jax.random.uniform(ks[12], (DEPTH, SSM_HEADS), f32, 1.0, 16.0)),
        'd_skip': 1.0 + nrm(ks[13], (DEPTH, SSM_HEADS), 0.02),
        'ssm_norm_g': 1.0 + nrm(ks[14], (DEPTH, SSM_DINNER), 0.02),
        'w_oa': nrm(ks[15], (DEPTH, GLA_DV, D_MODEL), GLA_DV ** -0.5),
        'w_ob': nrm(ks[16], (DEPTH, SSM_DINNER, D_MODEL), SSM_DINNER ** -0.5),
        'w_out': nrm(ks[17], (DEPTH, D_MODEL, D_MODEL), D_MODEL ** -0.5),
        'norm2_g': 1.0 + nrm(ks[18], (DEPTH, D_MODEL), 0.02),
        'router_group_w': nrm(ks[19], (DEPTH, D_MODEL, MOE_GROUPS), D_MODEL ** -0.5),
        'router_group_b': nrm(ks[20], (DEPTH, MOE_GROUPS), 0.01),
        'router_expert_w': nrm(ks[21], (DEPTH, D_MODEL, MOE_EXPERTS), D_MODEL ** -0.5),
        'router_expert_b': nrm(ks[22], (DEPTH, MOE_EXPERTS), 0.01),
        'expert_w1': nrm(ks[23], (DEPTH, MOE_GROUPS, MOE_EPG, D_MODEL, MOE_DFF), D_MODEL ** -0.5),
        'expert_w3': nrm(ks[24], (DEPTH, MOE_GROUPS, MOE_EPG, D_MODEL, MOE_DFF), D_MODEL ** -0.5),
        'expert_w2': nrm(ks[25], (DEPTH, MOE_GROUPS, MOE_EPG, MOE_DFF, D_MODEL), MOE_DFF ** -0.5),
        'final_norm_g': 1.0 + nrm(ks[26], (D_MODEL,), 0.02),
    }


def reference(x, c, ada_w, ada_b, norm1_g, w_in, gla_w_gk2, gla_b_gk, gla_norm_g, conv_w, conv_b,
              dt_bias, a_log, d_skip, ssm_norm_g, w_oa, w_ob, w_out, norm2_g, router_group_w,
              router_group_b, router_expert_w, router_expert_b, expert_w1, expert_w3, expert_w2,
              final_norm_g):
    c_act = jax.nn.silu(c)
    for l in range(DEPTH):
        mod = c_act @ ada_w[l] + ada_b[l]
        sh1, sc1, gt1, sh2, sc2, gt2 = [m[:, None, :] for m in jnp.split(mod, N_MOD, axis=-1)]
        h = rms_norm(x, norm1_g[l]) * (1.0 + sc1) + sh1
        proj = h @ w_in[l]
        q, k, v, g_out, gk_low, z, xbc, dt_raw, gates = split_cols(proj, IN_SIZES)
        o_a = gla_mixer(q, k, v, g_out, gk_low, gla_w_gk2[l], gla_b_gk[l], gla_norm_g[l])
        o_b = ssd_mixer(z, xbc, dt_raw, conv_w[l], conv_b[l], dt_bias[l], a_log[l], d_skip[l], ssm_norm_g[l])
        gate_a, gate_b = jnp.split(jax.nn.sigmoid(gates), N_BRANCH, axis=-1)
        merged = gate_a * (o_a @ w_oa[l]) + gate_b * (o_b @ w_ob[l])
        x = x + gt1 * (merged @ w_out[l])
        h = rms_norm(x, norm2_g[l]) * (1.0 + sc2) + sh2
        x = x + gt2 * hier_moe(h, router_group_w[l], router_group_b[l], router_expert_w[l],
                               router_expert_b[l], expert_w1[l], expert_w3[l], expert_w2[l])
    return rms_norm(x, final_norm_g)
```

```python
import functools

import jax
import jax.numpy as jnp
from jax import lax
from jax.experimental import pallas as pl
from jax.experimental.pallas import tpu as pltpu

F32 = jnp.float32
BF16 = jnp.bfloat16

EPS = 1e-6
CHUNK = 64
D_MODEL = 1024
N_MOD = 6
GLA_HEADS = 4
GLA_HK = 128
GLA_HV = 256
GLA_DK = GLA_HEADS * GLA_HK
GLA_DV = GLA_HEADS * GLA_HV
GLA_RANK = 16
GLA_NORMALIZER = 16.0
SSM_DINNER = 2048
SSM_HEADDIM = 64
SSM_HEADS = SSM_DINNER // SSM_HEADDIM
SSM_GROUPS = 8
SSM_HPG = SSM_HEADS // SSM_GROUPS
SSM_GW = SSM_HPG * SSM_HEADDIM
SSM_DSTATE = 128
SSM_CONV = 4
MOE_GROUPS = 4
MOE_EPG = 8
MOE_EXPERTS = MOE_GROUPS * MOE_EPG
MOE_DFF = 256

OFF_Q = 0
OFF_K = OFF_Q + GLA_DK
OFF_V = OFF_K + GLA_DK
OFF_G = OFF_V + GLA_DV
OFF_Z = OFF_G + GLA_DV
OFF_XS = OFF_Z + SSM_DINNER
OFF_B = OFF_XS + SSM_DINNER
OFF_C = OFF_B + SSM_GROUPS * SSM_DSTATE
OFF_GA = OFF_C + SSM_GROUPS * SSM_DSTATE
OFF_GB = OFF_GA + D_MODEL
N_PROJ = OFF_GB + D_MODEL
LANES = 128
SM_DT = GLA_RANK
RL_E = MOE_GROUPS

V7X_VMEM_LIMIT = 56 * 1024 * 1024

TM_PROJ = 1024
TN_PROJ = 1024
TM_MERGE = 512
TM_EXPERT = 256
TM_COMBINE = 256
DISPATCH_BLK = 1024
CONV_BLK = 256


def _cparams(sem):
    return pltpu.CompilerParams(dimension_semantics=sem, vmem_limit_bytes=V7X_VMEM_LIMIT)


def _sig(x):
    return 1.0 / (1.0 + jnp.exp(-x))


def _softplus(x):
    return jnp.maximum(x, 0.0) + jnp.log(1.0 + jnp.exp(-jnp.abs(x)))


def _log_sigmoid(x):
    return jnp.minimum(x, 0.0) - jnp.log(1.0 + jnp.exp(-jnp.abs(x)))


def _split(a):
    hi = a.astype(BF16)
    lo = (a - hi.astype(F32)).astype(BF16)
    return hi, lo


def _mm(a, b):
    return jnp.dot(a, b, preferred_element_type=F32)


def _dot3(a, b):
    ah, al = _split(a)
    bh, bl = _split(b)
    return _mm(ah, bh) + _mm(al, bh) + _mm(ah, bl)


def _dot2_rhs(a_bf16, b):
    bh, bl = _split(b)
    return _mm(a_bf16, bh) + _mm(a_bf16, bl)


def _dot2_lhs(a, b_bf16):
    ah, al = _split(a)
    return _mm(ah, b_bf16) + _mm(al, b_bf16)


def _tri_incl(n):
    r = lax.broadcasted_iota(jnp.int32, (n, n), 0)
    c = lax.broadcasted_iota(jnp.int32, (n, n), 1)
    return jnp.where(r >= c, 1.0, 0.0).astype(BF16)


def _rms(x):
    return x * lax.rsqrt(jnp.mean(x * x, axis=-1, keepdims=True) + EPS)


def _mod_kernel(c_ref, w_ref, b_ref, o_ref):
    c = c_ref[...]
    o_ref[0] = _dot3(c * _sig(c), w_ref[0]) + b_ref[0]


def _modulation(c, ada_w, ada_b):
    depth, d, n = ada_w.shape
    bsz = c.shape[0]
    tn = 1536
    return pl.pallas_call(
        _mod_kernel,
        grid=(depth, n // tn),
        in_specs=[
            pl.BlockSpec((bsz, d), lambda l, j: (0, 0)),
            pl.BlockSpec((1, d, tn), lambda l, j: (l, 0, j)),
            pl.BlockSpec((1, 1, tn), lambda l, j: (l, 0, j)),
        ],
        out_specs=pl.BlockSpec((1, bsz, tn), lambda l, j: (l, 0, j)),
        out_shape=jax.ShapeDtypeStruct((depth, bsz, n), F32),
        compiler_params=_cparams(("arbitrary", "arbitrary")),
        name="adaln_mod",
    )(c, ada_w, ada_b.reshape(depth, 1, n))


def _norm_proj_kernel(x_ref, g_ref, sc_ref, sh_ref, w_ref, ws_ref, o_ref, os_ref, h_s):
    @pl.when(pl.program_id(1) == 0)
    def _():
        h = _rms(x_ref[...]) * g_ref[...] * (1.0 + sc_ref[0]) + sh_ref[0]
        h_s[...] = h.astype(BF16)
        os_ref[...] = _dot3(h, ws_ref[...])

    o_ref[...] = _mm(h_s[...], w_ref[...]).astype(BF16)


def _norm_proj(x, g, sc, sh, w_main, w_small, seq):
    t, d = x.shape
    tpb = seq // TM_PROJ
    return pl.pallas_call(
        _norm_proj_kernel,
        grid=(t // TM_PROJ, N_PROJ // TN_PROJ),
        in_specs=[
            pl.BlockSpec((TM_PROJ, d), lambda i, j: (i, 0)),
            pl.BlockSpec((1, d), lambda i, j: (0, 0)),
            pl.BlockSpec((1, 1, d), lambda i, j: (i // tpb, 0, 0)),
            pl.BlockSpec((1, 1, d), lambda i, j: (i // tpb, 0, 0)),
            pl.BlockSpec((d, TN_PROJ), lambda i, j: (0, j)),
            pl.BlockSpec((d, LANES), lambda i, j: (0, 0)),
        ],
        out_specs=[
            pl.BlockSpec((TM_PROJ, TN_PROJ), lambda i, j: (i, j)),
            pl.BlockSpec((TM_PROJ, LANES), lambda i, j: (i, 0)),
        ],
        out_shape=[
            jax.ShapeDtypeStruct((t, N_PROJ), BF16),
            jax.ShapeDtypeStruct((t, LANES), F32),
        ],
        scratch_shapes=[pltpu.VMEM((TM_PROJ, d), BF16)],
        compiler_params=_cparams(("arbitrary", "arbitrary")),
        name="norm_in_proj",
    )(x, g, sc, sh, w_main, w_small)


def _gla_kernel(q_ref, k_ref, v_ref, g_ref, sm_ref, wg_ref, bg_ref, ng_ref, o_ref, gk_s, st_s):
    seq = q_ref.shape[0]
    pre = _dot3(sm_ref[...], wg_ref[...]) + bg_ref[...]
    gk_s[...] = _log_sigmoid(pre) * (1.0 / GLA_NORMALIZER)
    st_s[...] = jnp.zeros_like(st_s)
    tri = _tri_incl(CHUNK)
    ng = ng_ref[...]
    scale = GLA_HK ** -0.5

    def body(n, carry):
        rows = pl.ds(pl.multiple_of(n * CHUNK, CHUNK), CHUNK)
        cum = _dot2_rhs(tri, gk_s[rows, :])
        gend = cum[CHUNK - 1:CHUNK, :]
        kdec = (k_ref[rows, :].astype(F32) * jnp.exp(gend - cum)).astype(BF16)
        upd = lax.dot_general(v_ref[rows, :], kdec, (((0,), (0,)), ((), ())),
                              preferred_element_type=F32)
        st = jnp.exp(gend) * st_s[...] + upd
        st_s[...] = st
        o = lax.dot_general(q_ref[rows, :], st.astype(BF16), (((1,), (1,)), ((), ())),
                            preferred_element_type=F32) * scale
        o = _rms(o) * ng
        g = g_ref[rows, :].astype(F32)
        o_ref[rows, :] = (o * (g * _sig(g))).astype(BF16)
        return carry

    lax.fori_loop(0, seq // CHUNK, body, 0, unroll=2)


def _gla(proj, small, wg_pad, b_gk, norm_g, bsz, seq):
    t = proj.shape[0]
    kq, kk = OFF_Q // GLA_HK, OFF_K // GLA_HK
    kv, kg = OFF_V // GLA_HV, OFF_G // GLA_HV
    return pl.pallas_call(
        _gla_kernel,
        grid=(bsz, GLA_HEADS),
        in_specs=[
            pl.BlockSpec((seq, GLA_HK), lambda b, h: (b, kq + h)),
            pl.BlockSpec((seq, GLA_HK), lambda b, h: (b, kk + h)),
            pl.BlockSpec((seq, GLA_HV), lambda b, h: (b, kv + h)),
            pl.BlockSpec((seq, GLA_HV), lambda b, h: (b, kg + h)),
            pl.BlockSpec((seq, LANES), lambda b, h: (b, 0)),
            pl.BlockSpec((LANES, GLA_HK), lambda b, h: (0, h)),
            pl.BlockSpec((1, GLA_HK), lambda b, h: (0, h)),
            pl.BlockSpec((1, GLA_HV), lambda b, h: (0, 0)),
        ],
        out_specs=pl.BlockSpec((seq, GLA_HV), lambda b, h: (b, h)),
        out_shape=jax.ShapeDtypeStruct((t, GLA_DV), BF16),
        scratch_shapes=[pltpu.VMEM((seq, GLA_HK), F32), pltpu.VMEM((GLA_HV, GLA_HK), F32)],
        compiler_params=_cparams(("arbitrary", "arbitrary")),
        name="gla_mixer",
    )(proj, proj, proj, proj, small, wg_pad, b_gk, norm_g)


def _conv_silu(u, halo, w_ref, b_ref):
    n = u.shape[0]
    win = jnp.concatenate([halo, u], axis=0)
    acc = u * w_ref[SSM_CONV - 1:SSM_CONV, :] + b_ref[...]
    for k in range(1, SSM_CONV):
        acc = acc + pltpu.roll(win, k, 0)[8:8 + n, :] * w_ref[SSM_CONV - 1 - k:SSM_CONV - k, :]
    return acc * _sig(acc)


def _ssd_kernel(z_ref, xs_ref, b_ref, c_ref, sm_ref, cwx_ref, cwb_ref, cwc_ref, cbx_ref, cbb_ref,
                cbc_ref, dtb_ref, alog_ref, dsk_ref, ng_ref, o_ref, xs_s, b_s, c_s, dt_s, st_s):
    seq = z_ref.shape[0]
    grp = pl.program_id(1)
    er = lax.broadcasted_iota(jnp.int32, (LANES, SSM_GW), 0)
    el = lax.broadcasted_iota(jnp.int32, (LANES, SSM_GW), 1)
    expand = jnp.where(er == SM_DT + SSM_HPG * grp + el // SSM_HEADDIM, 1.0, 0.0).astype(BF16)

    def prep(i, carry):
        hx, hb, hc = carry
        rows = pl.ds(pl.multiple_of(i * CONV_BLK, CONV_BLK), CONV_BLK)
        dt_s[rows, :] = _softplus(_dot2_lhs(sm_ref[rows, :], expand) + dtb_ref[...])
        ux = xs_ref[rows, :].astype(F32)
        ub = b_ref[rows, :].astype(F32)
        uc = c_ref[rows, :].astype(F32)
        xs_s[rows, :] = _conv_silu(ux, hx, cwx_ref, cbx_ref)
        b_s[rows, :] = _conv_silu(ub, hb, cwb_ref, cbb_ref).astype(BF16)
        c_s[rows, :] = _conv_silu(uc, hc, cwc_ref, cbc_ref).astype(BF16)
        return ux[CONV_BLK - 8:, :], ub[CONV_BLK - 8:, :], uc[CONV_BLK - 8:, :]

    lax.fori_loop(0, seq // CONV_BLK, prep,
                  (jnp.zeros((8, SSM_GW), F32), jnp.zeros((8, SSM_DSTATE), F32),
                   jnp.zeros((8, SSM_DSTATE), F32)))

    st_s[...] = jnp.zeros_like(st_s)
    tri = _tri_incl(CHUNK)
    a_row = -jnp.exp(alog_ref[...])
    dsk = dsk_ref[...]
    ng = ng_ref[...]
    dr = lax.broadcasted_iota(jnp.int32, (CHUNK, SSM_GW), 0)
    dl = lax.broadcasted_iota(jnp.int32, (CHUNK, SSM_GW), 1)
    diag = jnp.where(dr == dl % CHUNK, 1.0, 0.0)
    br = lax.broadcasted_iota(jnp.int32, (SSM_GW, SSM_GW), 0)
    bl = lax.broadcasted_iota(jnp.int32, (SSM_GW, SSM_GW), 1)
    blockmask = jnp.where(br // SSM_HEADDIM == bl // SSM_HEADDIM, 1.0, 0.0).astype(BF16)

    def body(n, carry):
        rows = pl.ds(pl.multiple_of(n * CHUNK, CHUNK), CHUNK)
        dt = dt_s[rows, :]
        acum = _dot2_rhs(tri, dt * a_row)
        aend = acum[CHUNK - 1:CHUNK, :]
        xs = xs_s[rows, :]
        xdt = xs * dt
        bm = b_s[rows, :]
        cm = c_s[rows, :]
        acum_s = jnp.sum(acum * diag, axis=0, keepdims=True)
        lmask = jnp.exp(-jnp.abs(acum - acum_s))
        cb = lax.dot_general(cm, jnp.concatenate([bm] * SSM_HPG, axis=0), (((1,), (1,)), ((), ())),
                             preferred_element_type=F32)
        xdt_b = xdt.astype(BF16)
        xdt_bd = jnp.concatenate([xdt_b] * SSM_HPG, axis=0) * blockmask
        y = _mm((cb * lmask).astype(BF16), xdt_bd)
        prev = st_s[...]
        y = y + _mm(cm, prev.astype(BF16)) * jnp.exp(acum)
        upd = lax.dot_general(bm, (xdt * jnp.exp(aend - acum)).astype(BF16),
                              (((0,), (0,)), ((), ())), preferred_element_type=F32)
        st_s[...] = jnp.exp(aend) * prev + upd
        y = y + xs * dsk
        z = z_ref[rows, :].astype(F32)
        y = y * (z * _sig(z))
        o_ref[rows, :] = (_rms(y) * ng).astype(BF16)
        return carry

    lax.fori_loop(0, seq // CHUNK, body, 0, unroll=2)


def _ssd(proj, small, conv_w, conv_b, dt_bias_e, a_log_e, d_skip_e, norm_g, bsz, seq):
    t = proj.shape[0]
    kz, kx = OFF_Z // SSM_GW, OFF_XS // SSM_GW
    kb, kc = OFF_B // SSM_DSTATE, OFF_C // SSM_DSTATE
    nb = SSM_DINNER // SSM_DSTATE
    ncg = SSM_GROUPS
    return pl.pallas_call(
        _ssd_kernel,
        grid=(bsz, SSM_GROUPS),
        in_specs=[
            pl.BlockSpec((seq, SSM_GW), lambda b, g: (b, kz + g)),
            pl.BlockSpec((seq, SSM_GW), lambda b, g: (b, kx + g)),
            pl.BlockSpec((seq, SSM_DSTATE), lambda b, g: (b, kb + g)),
            pl.BlockSpec((seq, SSM_DSTATE), lambda b, g: (b, kc + g)),
            pl.BlockSpec((seq, LANES), lambda b, g: (b, 0)),
            pl.BlockSpec((SSM_CONV, SSM_GW), lambda b, g: (0, g)),
            pl.BlockSpec((SSM_CONV, SSM_DSTATE), lambda b, g: (0, nb + g)),
            pl.BlockSpec((SSM_CONV, SSM_DSTATE), lambda b, g: (0, nb + ncg + g)),
            pl.BlockSpec((1, SSM_GW), lambda b, g: (0, g)),
            pl.BlockSpec((1, SSM_DSTATE), lambda b, g: (0, nb + g)),
            pl.BlockSpec((1, SSM_DSTATE), lambda b, g: (0, nb + ncg + g)),
            pl.BlockSpec((1, SSM_GW), lambda b, g: (0, g)),
            pl.BlockSpec((1, SSM_GW), lambda b, g: (0, g)),
            pl.BlockSpec((1, SSM_GW), lambda b, g: (0, g)),
            pl.BlockSpec((1, SSM_GW), lambda b, g: (0, g)),
        ],
        out_specs=pl.BlockSpec((seq, SSM_GW), lambda b, g: (b, g)),
        out_shape=jax.ShapeDtypeStruct((t, SSM_DINNER), BF16),
        scratch_shapes=[
            pltpu.VMEM((seq, SSM_GW), F32),
            pltpu.VMEM((seq, SSM_DSTATE), BF16),
            pltpu.VMEM((seq, SSM_DSTATE), BF16),
            pltpu.VMEM((seq, SSM_GW), F32),
            pltpu.VMEM((SSM_DSTATE, SSM_GW), F32),
        ],
        compiler_params=_cparams(("arbitrary", "arbitrary")),
        name="ssd_mixer",
    )(proj, proj, proj, proj, small, conv_w, conv_w, conv_w, conv_b, conv_b, conv_b,
      dt_bias_e, a_log_e, d_skip_e, norm_g)


def _lane_min(x):
    return jnp.min(x, axis=-1, keepdims=True)


def _lane_max(x):
    return jnp.max(x, axis=-1, keepdims=True)


def _merge_kernel(oa_ref, ob_ref, ga_ref, gb_ref, x_ref, gt_ref, woa_ref, wob_ref, wout_ref,
                  g2_ref, sc_ref, sh_ref, wr_ref, br_ref,
                  x1_ref, h2_ref, ri_ref, rw_ref, cnt_ref, cnt_s):
    i = pl.program_id(0)
    tm = x_ref.shape[0]

    @pl.when(i == 0)
    def _():
        cnt_s[...] = jnp.zeros_like(cnt_s)

    ya = _mm(oa_ref[...], woa_ref[...])
    yb = _mm(ob_ref[...], wob_ref[...])
    merged = _sig(ga_ref[...].astype(F32)) * ya + _sig(gb_ref[...].astype(F32)) * yb
    x1 = x_ref[...] + gt_ref[0] * _mm(merged.astype(BF16), wout_ref[...])
    x1_ref[...] = x1
    h = _rms(x1) * g2_ref[...] * (1.0 + sc_ref[0]) + sh_ref[0]
    h2_ref[...] = h

    logit = _dot3(h, wr_ref[...]) + br_ref[...]
    lane = lax.broadcasted_iota(jnp.int32, (tm, LANES), 1)
    lane_f = lane.astype(F32)
    neg = jnp.float32(-jnp.inf)
    big = jnp.float32(LANES)
    is_g = lane < MOE_GROUPS
    lg = jnp.where(is_g, logit, neg)
    gmax = _lane_max(lg)
    gsum = jnp.sum(jnp.where(is_g, jnp.exp(lg - gmax), 0.0), axis=-1, keepdims=True)
    gidx = _lane_min(jnp.where(lg == gmax, lane_f, big))
    lo = RL_E + MOE_EPG * gidx
    in_grp = (lane_f >= lo) & (lane_f < lo + MOE_EPG)
    le = jnp.where(in_grp, logit, neg)
    emax = _lane_max(le)
    l0 = _lane_min(jnp.where(le == emax, lane_f, big))
    le2 = jnp.where(lane_f == l0, neg, le)
    emax2 = _lane_max(le2)
    l1 = _lane_min(jnp.where(le2 == emax2, lane_f, big))
    ratio = jnp.exp(emax2 - emax)
    w0 = 1.0 / (gsum * (1.0 + ratio))
    w1 = w0 * ratio

    oh0 = lane_f == l0
    oh1 = lane_f == l1
    oh = jnp.where(oh0 | oh1, 1.0, 0.0)
    tr = lax.broadcasted_iota(jnp.int32, (tm, tm), 0)
    tc = lax.broadcasted_iota(jnp.int32, (tm, tm), 1)
    before = _mm(jnp.where(tr > tc, 1.0, 0.0).astype(BF16), oh.astype(BF16)) + cnt_s[0:1, :]
    r0 = jnp.sum(jnp.where(oh0, before, 0.0), axis=-1, keepdims=True)
    r1 = jnp.sum(jnp.where(oh1, before, 0.0), axis=-1, keepdims=True)
    total = cnt_s[0:1, :] + jnp.sum(oh, axis=0, keepdims=True)
    cnt_s[...] = jnp.broadcast_to(total, cnt_s.shape)
    cnt_ref[...] = jnp.broadcast_to(total, cnt_ref.shape)

    ri = jnp.where(lane == 0, l0 - RL_E,
                   jnp.where(lane == 1, l1 - RL_E, jnp.where(lane == 2, r0, jnp.where(lane == 3, r1, 0.0))))
    ri_ref[...] = ri.astype(jnp.int32)
    rw_ref[...] = jnp.where(lane == 0, w0, jnp.where(lane == 1, w1, 0.0))


def _merge(o_a, o_b, proj, x, gt1, w_oa, w_ob, w_out, g2, sc2, sh2, w_r, b_r, seq):
    t, d = x.shape
    tm = TM_MERGE
    tpb = seq // tm
    kga, kgb = OFF_GA // d, OFF_GB // d
    const = lambda i: (0, 0)
    return pl.pallas_call(
        _merge_kernel,
        grid=(t // tm,),
        in_specs=[
            pl.BlockSpec((tm, GLA_DV), lambda i: (i, 0)),
            pl.BlockSpec((tm, SSM_DINNER), lambda i: (i, 0)),
            pl.BlockSpec((tm, d), lambda i: (i, kga)),
            pl.BlockSpec((tm, d), lambda i: (i, kgb)),
            pl.BlockSpec((tm, d), lambda i: (i, 0)),
            pl.BlockSpec((1, 1, d), lambda i: (i // tpb, 0, 0)),
            pl.BlockSpec((GLA_DV, d), const),
            pl.BlockSpec((SSM_DINNER, d), const),
            pl.BlockSpec((d, d), const),
            pl.BlockSpec((1, d), const),
            pl.BlockSpec((1, 1, d), lambda i: (i // tpb, 0, 0)),
            pl.BlockSpec((1, 1, d), lambda i: (i // tpb, 0, 0)),
            pl.BlockSpec((d, LANES), const),
            pl.BlockSpec((1, LANES), const),
        ],
        out_specs=[
            pl.BlockSpec((tm, d), lambda i: (i, 0)),
            pl.BlockSpec((tm, d), lambda i: (i, 0)),
            pl.BlockSpec((tm, LANES), lambda i: (i, 0)),
            pl.BlockSpec((tm, LANES), lambda i: (i, 0)),
            pl.BlockSpec((8, LANES), const),
        ],
        out_shape=[
            jax.ShapeDtypeStruct((t, d), F32),
            jax.ShapeDtypeStruct((t, d), F32),
            jax.ShapeDtypeStruct((t, LANES), jnp.int32),
            jax.ShapeDtypeStruct((t, LANES), F32),
            jax.ShapeDtypeStruct((8, LANES), F32),
        ],
        scratch_shapes=[pltpu.VMEM((8, LANES), F32)],
        compiler_params=_cparams(("arbitrary",)),
        name="merge_router",
    )(o_a, o_b, proj, proj, x, gt1, w_oa, w_ob, w_out, g2, sc2, sh2, w_r, b_r)


def _row_copy(src_ref, src_row, dst_ref, dst_row, sem):
    return pltpu.make_async_copy(src_ref.at[pl.ds(src_row, 1)], dst_ref.at[pl.ds(dst_row, 1)], sem)


def _dispatch_kernel(pos_ref, h_ref, init_ref, xs_ref, sem):
    del init_ref
    base = pl.program_id(0) * DISPATCH_BLK

    def issue(tok, carry):
        _row_copy(h_ref, base + tok, xs_ref, pos_ref[2 * tok], sem).start()
        _row_copy(h_ref, base + tok, xs_ref, pos_ref[2 * tok + 1], sem).start()
        return carry

    lax.fori_loop(0, DISPATCH_BLK, issue, 0, unroll=8)

    def drain(tok, carry):
        _row_copy(h_ref, 0, xs_ref, 0, sem).wait()
        _row_copy(h_ref, 0, xs_ref, 0, sem).wait()
        return carry

    lax.fori_loop(0, DISPATCH_BLK, drain, 0, unroll=8)


def _dispatch(pos_flat, h2, n_rows):
    t, d = h2.shape
    return pl.pallas_call(
        _dispatch_kernel,
        grid=(t // DISPATCH_BLK,),
        in_specs=[
            pl.BlockSpec((2 * DISPATCH_BLK,), lambda i: (i,), memory_space=pltpu.SMEM),
            pl.BlockSpec(memory_space=pl.ANY),
            pl.BlockSpec(memory_space=pl.ANY),
        ],
        out_specs=pl.BlockSpec(memory_space=pl.ANY),
        out_shape=jax.ShapeDtypeStruct((n_rows, d), F32),
        scratch_shapes=[pltpu.SemaphoreType.DMA(())],
        input_output_aliases={2: 0},
        compiler_params=_cparams(("arbitrary",)),
        name="moe_dispatch",
    )(pos_flat, h2, jnp.zeros((n_rows, d), F32))


def _expert_kernel(te_ref, nv_ref, x_ref, w1_ref, w3_ref, w2_ref, o_ref):
    del te_ref
    valid = pl.program_id(0) < nv_ref[0]

    @pl.when(valid)
    def _():
        x = x_ref[...].astype(BF16)
        a = _mm(x, w1_ref[0])
        b = _mm(x, w3_ref[0])
        o_ref[...] = _mm((a * _sig(a) * b).astype(BF16), w2_ref[0])

    @pl.when(jnp.logical_not(valid))
    def _():
        o_ref[...] = jnp.zeros_like(o_ref)


def _experts(tile_expert, n_valid, xs, w1, w3, w2):
    n_rows, d = xs.shape
    tm = TM_EXPERT
    grid_spec = pltpu.PrefetchScalarGridSpec(
        num_scalar_prefetch=2,
        grid=(n_rows // tm,),
        in_specs=[
            pl.BlockSpec((tm, d), lambda i, te, nv: (jnp.minimum(i, nv[0] - 1), 0)),
            pl.BlockSpec((1, d, MOE_DFF), lambda i, te, nv: (te[i], 0, 0)),
            pl.BlockSpec((1, d, MOE_DFF), lambda i, te, nv: (te[i], 0, 0)),
            pl.BlockSpec((1, MOE_DFF, d), lambda i, te, nv: (te[i], 0, 0)),
        ],
        out_specs=pl.BlockSpec((tm, d), lambda i, te, nv: (i, 0)),
    )
    return pl.pallas_call(
        _expert_kernel,
        grid_spec=grid_spec,
        out_shape=jax.ShapeDtypeStruct((n_rows, d), F32),
        compiler_params=_cparams(("arbitrary",)),
        name="moe_experts",
    )(tile_expert, n_valid, xs, w1, w3, w2)


def _combine_kernel(pos_ref, ys_ref, x_ref, rw_ref, gt_ref, fg_ref, o_ref, buf, sem, *, final):
    tm = x_ref.shape[0]

    def issue(tok, carry):
        _row_copy(ys_ref, pos_ref[2 * tok], buf.at[0], tok, sem).start()
        _row_copy(ys_ref, pos_ref[2 * tok + 1], buf.at[1], tok, sem).start()
        return carry

    lax.fori_loop(0, tm, issue, 0, unroll=8)

    def drain(tok, carry):
        _row_copy(ys_ref, 0, buf.at[0], 0, sem).wait()
        _row_copy(ys_ref, 0, buf.at[1], 0, sem).wait()
        return carry

    lax.fori_loop(0, tm, drain, 0, unroll=8)

    rw = rw_ref[...]
    y = rw[:, 0:1] * buf[0] + rw[:, 1:2] * buf[1]
    out = x_ref[...] + gt_ref[0] * y
    if final:
        out = _rms(out) * fg_ref[...]
    o_ref[...] = out


def _combine(pos_flat, ys, x1, rw, gt2, final_g, seq, final):
    t, d = x1.shape
    tm = TM_COMBINE
    tpb = seq // tm
    return pl.pallas_call(
        functools.partial(_combine_kernel, final=final),
        grid=(t // tm,),
        in_specs=[
            pl.BlockSpec((2 * tm,), lambda i: (i,), memory_space=pltpu.SMEM),
            pl.BlockSpec(memory_space=pl.ANY),
            pl.BlockSpec((tm, d), lambda i: (i, 0)),
            pl.BlockSpec((tm, LANES), lambda i: (i, 0)),
            pl.BlockSpec((1, 1, d), lambda i: (i // tpb, 0, 0)),
            pl.BlockSpec((1, d), lambda i: (0, 0)),
        ],
        out_specs=pl.BlockSpec((tm, d), lambda i: (i, 0)),
        out_shape=jax.ShapeDtypeStruct((t, d), F32),
        scratch_shapes=[pltpu.VMEM((2, tm, d), F32), pltpu.SemaphoreType.DMA(())],
        compiler_params=_cparams(("arbitrary",)),
        name="moe_combine",
    )(pos_flat, ys, x1, rw, gt2, final_g)


def _routing_tables(ri, cnt, n_tiles):
    counts = cnt[0, RL_E:RL_E + MOE_EXPERTS].astype(jnp.int32)
    padded = ((counts + TM_EXPERT - 1) // TM_EXPERT) * TM_EXPERT
    ends = jnp.cumsum(padded)
    starts = ends - padded
    pos = jnp.stack([starts[ri[:, 0]] + ri[:, 2], starts[ri[:, 1]] + ri[:, 3]], axis=1).reshape(-1)
    n_valid = (ends[-1] // TM_EXPERT).astype(jnp.int32).reshape(1)
    tile_start = jnp.arange(n_tiles, dtype=jnp.int32) * TM_EXPERT
    tile_expert = jnp.minimum(jnp.searchsorted(ends, tile_start, side="right"),
                              MOE_EXPERTS - 1).astype(jnp.int32)
    return pos.astype(jnp.int32), tile_expert, n_valid


def kernel(x, c, ada_w, ada_b, norm1_g, w_in, gla_w_gk2, gla_b_gk, gla_norm_g, conv_w, conv_b,
           dt_bias, a_log, d_skip, ssm_norm_g, w_oa, w_ob, w_out, norm2_g, router_group_w,
           router_group_b, router_expert_w, router_expert_b, expert_w1, expert_w3, expert_w2,
           final_norm_g):
    bsz, seq, d = x.shape
    depth = ada_w.shape[0]
    t = bsz * seq
    n_tiles = (2 * t) // TM_EXPERT + MOE_EXPERTS
    n_rows = n_tiles * TM_EXPERT

    mod = _modulation(c, ada_w, ada_b)
    xf = x.reshape(t, d)
    c_gk, c_dt = OFF_G + GLA_DV, OFF_C + SSM_GROUPS * SSM_DSTATE + GLA_RANK
    for l in range(depth):
        sh1, sc1, gt1, sh2, sc2, gt2 = [mod[l, :, k * d:(k + 1) * d].reshape(bsz, 1, d) for k in range(N_MOD)]
        w = w_in[l]
        w_main = jnp.concatenate(
            [w[:, :c_gk], w[:, c_gk + GLA_RANK:c_dt], w[:, c_dt + SSM_HEADS:]], axis=1).astype(BF16)
        w_small = jnp.concatenate(
            [w[:, c_gk:c_gk + GLA_RANK], w[:, c_dt:c_dt + SSM_HEADS],
             jnp.zeros((d, LANES - GLA_RANK - SSM_HEADS), F32)], axis=1)
        proj, small = _norm_proj(xf, norm1_g[l].reshape(1, d), sc1, sh1, w_main, w_small, seq)

        wg_pad = jnp.concatenate([gla_w_gk2[l], jnp.zeros((LANES - GLA_RANK, GLA_DK), F32)], axis=0)
        o_a = _gla(proj, small, wg_pad, gla_b_gk[l].reshape(1, GLA_DK),
                   gla_norm_g[l].reshape(1, GLA_HV), bsz, seq)
        o_b = _ssd(proj, small, conv_w[l], conv_b[l].reshape(1, -1),
                   jnp.repeat(dt_bias[l], SSM_HEADDIM).reshape(1, SSM_DINNER),
                   jnp.repeat(a_log[l], SSM_HEADDIM).reshape(1, SSM_DINNER),
                   jnp.repeat(d_skip[l], SSM_HEADDIM).reshape(1, SSM_DINNER),
                   ssm_norm_g[l].reshape(1, SSM_DINNER), bsz, seq)

        w_r = jnp.concatenate([router_group_w[l], router_expert_w[l],
                               jnp.zeros((d, LANES - MOE_GROUPS - MOE_EXPERTS), F32)], axis=1)
        b_r = jnp.concatenate([router_group_b[l], router_expert_b[l],
                               jnp.zeros((LANES - MOE_GROUPS - MOE_EXPERTS,), F32)]).reshape(1, LANES)
        x1, h2, ri, rw, cnt = _merge(o_a, o_b, proj, xf, gt1, w_oa[l].astype(BF16), w_ob[l].astype(BF16),
                                     w_out[l].astype(BF16), norm2_g[l].reshape(1, d), sc2, sh2, w_r, b_r, seq)

        pos, tile_expert, n_valid = _routing_tables(ri, cnt, n_tiles)
        xs = _dispatch(pos, h2, n_rows)
        ys = _experts(tile_expert, n_valid, xs,
                      expert_w1[l].reshape(MOE_EXPERTS, d, MOE_DFF).astype(BF16),
                      expert_w3[l].reshape(MOE_EXPERTS, d, MOE_DFF).astype(BF16),
                      expert_w2[l].reshape(MOE_EXPERTS, MOE_DFF, d).astype(BF16))
        xf = _combine(pos, ys, x1, rw, gt2, final_norm_g.reshape(1, d), seq, final=(l == depth - 1))
    return xf.reshape(bsz, seq, d)
```

```python
import functools

import jax
import jax.numpy as jnp
from jax import lax
from jax.experimental import pallas as pl
from jax.experimental.pallas import tpu as pltpu

F32 = jnp.float32
BF16 = jnp.bfloat16

EPS = 1e-6
CHUNK = 64
D_MODEL = 1024
N_MOD = 6
GLA_HEADS = 4
GLA_HK = 128
GLA_HV = 256
GLA_DK = GLA_HEADS * GLA_HK
GLA_DV = GLA_HEADS * GLA_HV
GLA_RANK = 16
GLA_NORMALIZER = 16.0
SSM_DINNER = 2048
SSM_HEADDIM = 64
SSM_HEADS = SSM_DINNER // SSM_HEADDIM
SSM_GROUPS = 8
SSM_HPG = SSM_HEADS // SSM_GROUPS
SSM_GW = SSM_HPG * SSM_HEADDIM
SSM_DSTATE = 128
SSM_CONV = 4
MOE_GROUPS = 4
MOE_EPG = 8
MOE_EXPERTS = MOE_GROUPS * MOE_EPG
MOE_DFF = 256

OFF_Q = 0
OFF_K = OFF_Q + GLA_DK
OFF_V = OFF_K + GLA_DK
OFF_G = OFF_V + GLA_DV
OFF_Z = OFF_G + GLA_DV
OFF_XS = OFF_Z + SSM_DINNER
OFF_B = OFF_XS + SSM_DINNER
OFF_C = OFF_B + SSM_GROUPS * SSM_DSTATE
OFF_GA = OFF_C + SSM_GROUPS * SSM_DSTATE
OFF_GB = OFF_GA + D_MODEL
N_PROJ = OFF_GB + D_MODEL
LANES = 128
SM_DT = GLA_RANK
RL_E = MOE_GROUPS

V7X_VMEM_LIMIT = 56 * 1024 * 1024

TM_PROJ = 1024
TN_PROJ = 1024
TM_MERGE = 512
TM_EXPERT = 256
TM_COMBINE = 256
TM_POS = 2048
DISPATCH_BLK = 1024
MIX_BLK = 256
CPB = MIX_BLK // CHUNK
CONV_ROWS = 128


def _cparams(sem):
    return pltpu.CompilerParams(dimension_semantics=sem, vmem_limit_bytes=V7X_VMEM_LIMIT)


def _sig(x):
    return 0.5 * jnp.tanh(0.5 * x) + 0.5


def _softplus(x):
    return jnp.maximum(x, 0.0) + jnp.log(1.0 + jnp.exp(-jnp.abs(x)))


def _log_sigmoid(x):
    return jnp.minimum(x, 0.0) - jnp.log(1.0 + jnp.exp(-jnp.abs(x)))


def _split(a):
    hi = a.astype(BF16)
    lo = (a - hi.astype(F32)).astype(BF16)
    return hi, lo


def _mm(a, b):
    return jnp.dot(a, b, preferred_element_type=F32)


def _dot3(a, b):
    ah, al = _split(a)
    bh, bl = _split(b)
    return _mm(ah, bh) + _mm(al, bh) + _mm(ah, bl)


def _dot2_lhs(a, b_bf16):
    ah, al = _split(a)
    return _mm(ah, b_bf16) + _mm(al, b_bf16)


def _chunk_scan_matrix(n):
    r = lax.broadcasted_iota(jnp.int32, (2 * n, n), 0)
    c = lax.broadcasted_iota(jnp.int32, (2 * n, n), 1)
    t = jnp.where(r >= n, r - n, r)
    keep = (t // CHUNK == c // CHUNK) & ((r >= n) | (t >= c))
    return jnp.where(keep, 1.0, 0.0).astype(BF16)


def _chunk_scan(mat, x):
    n, w = x.shape
    hi, lo = _split(x)
    out = _mm(mat, jnp.concatenate([hi, lo], axis=1))
    s = out[:, :w] + out[:, w:]
    return s[:n], s[n:]


def _rms(x):
    return x * lax.rsqrt(jnp.mean(x * x, axis=-1, keepdims=True) + EPS)


def _skewed_loop(n, stages):
    depth = len(stages)

    def step(i, lo, hi):
        for k in reversed(range(lo, hi)):
            stages[k](i - k)

    for i in range(depth - 1):
        step(i, 0, i + 1)

    def body(i, carry):
        step(i, 0, depth)
        return carry

    lax.fori_loop(depth - 1, n, body, 0)
    for i in range(n, n + depth - 1):
        step(i, i - n + 1, depth)


def _item_rows(i):
    start = i * MIX_BLK
    return start if isinstance(start, int) else pl.multiple_of(start, MIX_BLK)


def _mod_kernel(c_ref, w_ref, b_ref, o_ref):
    c = c_ref[...]
    o_ref[0] = _dot3(c * _sig(c), w_ref[0]) + b_ref[0]


def _modulation(c, ada_w, ada_b):
    depth, d, n = ada_w.shape
    bsz = c.shape[0]
    tn = 1536
    return pl.pallas_call(
        _mod_kernel,
        grid=(depth, n // tn),
        in_specs=[
            pl.BlockSpec((bsz, d), lambda l, j: (0, 0)),
            pl.BlockSpec((1, d, tn), lambda l, j: (l, 0, j)),
            pl.BlockSpec((1, 1, tn), lambda l, j: (l, 0, j)),
        ],
        out_specs=pl.BlockSpec((1, bsz, tn), lambda l, j: (l, 0, j)),
        out_shape=jax.ShapeDtypeStruct((depth, bsz, n), F32),
        compiler_params=_cparams(("arbitrary", "arbitrary")),
        name="adaln_mod",
    )(c, ada_w, ada_b.reshape(depth, 1, n))


def _norm_proj_kernel(x_ref, g_ref, sc_ref, sh_ref, w_ref, ws_ref, dtb_ref, o_ref, os_ref, h_s):
    @pl.when(pl.program_id(1) == 0)
    def _():
        h = _rms(x_ref[...]) * g_ref[...] * (1.0 + sc_ref[0]) + sh_ref[0]
        h_s[...] = h.astype(BF16)
        side = _dot3(h, ws_ref[...])
        lane = lax.broadcasted_iota(jnp.int32, side.shape, 1)
        is_dt = (lane >= SM_DT) & (lane < SM_DT + SSM_HEADS)
        os_ref[...] = jnp.where(is_dt, _softplus(side + dtb_ref[...]), side)

    o_ref[...] = _mm(h_s[...], w_ref[...]).astype(BF16)


def _norm_proj(x, g, sc, sh, w_main, w_small, dt_bias_pad, seq):
    t, d = x.shape
    tpb = seq // TM_PROJ
    return pl.pallas_call(
        _norm_proj_kernel,
        grid=(t // TM_PROJ, N_PROJ // TN_PROJ),
        in_specs=[
            pl.BlockSpec((TM_PROJ, d), lambda i, j: (i, 0)),
            pl.BlockSpec((1, d), lambda i, j: (0, 0)),
            pl.BlockSpec((1, 1, d), lambda i, j: (i // tpb, 0, 0)),
            pl.BlockSpec((1, 1, d), lambda i, j: (i // tpb, 0, 0)),
            pl.BlockSpec((d, TN_PROJ), lambda i, j: (0, j)),
            pl.BlockSpec((d, LANES), lambda i, j: (0, 0)),
            pl.BlockSpec((1, LANES), lambda i, j: (0, 0)),
        ],
        out_specs=[
            pl.BlockSpec((TM_PROJ, TN_PROJ), lambda i, j: (i, j)),
            pl.BlockSpec((TM_PROJ, LANES), lambda i, j: (i, 0)),
        ],
        out_shape=[
            jax.ShapeDtypeStruct((t, N_PROJ), BF16),
            jax.ShapeDtypeStruct((t, LANES), F32),
        ],
        scratch_shapes=[pltpu.VMEM((TM_PROJ, d), BF16)],
        compiler_params=_cparams(("arbitrary", "arbitrary")),
        name="norm_in_proj",
    )(x, g, sc, sh, w_main, w_small, dt_bias_pad)


def _gla_kernel(q_ref, k_ref, v_ref, g_ref, sm_ref, wg_ref, bg_ref, ng_ref, o_ref,
                gk_s, kdec_s, edec_s, upd_s, st_s, oraw_s):
    seq = q_ref.shape[0]
    scan = _chunk_scan_matrix(MIX_BLK)
    wg_hi, wg_lo = _split(wg_ref[...])
    bg = bg_ref[...]
    ng = ng_ref[...]
    scale = GLA_HK ** -0.5
    st_s[...] = jnp.zeros_like(st_s)

    def stage_gate(i):
        rows = pl.ds(_item_rows(i), MIX_BLK)
        sm_hi, sm_lo = _split(sm_ref[rows, :])
        pre = _mm(sm_hi, wg_hi) + _mm(sm_lo, wg_hi) + _mm(sm_hi, wg_lo) + bg
        gk_s[rows, :] = _log_sigmoid(pre) * (1.0 / GLA_NORMALIZER)

    def stage_decay(i):
        rows = pl.ds(_item_rows(i), MIX_BLK)
        cum, tot = _chunk_scan(scan, gk_s[rows, :])
        kdec_s[rows, :] = (k_ref[rows, :].astype(F32) * jnp.exp(tot - cum)).astype(BF16)
        edec_s[rows, :] = jnp.exp(tot)

    def stage_update(i):
        for c in range(CPB):
            rows = pl.ds(_item_rows(i) + c * CHUNK, CHUNK)
            upd_s[i * CPB + c] = lax.dot_general(
                v_ref[rows, :], kdec_s[rows, :], (((0,), (0,)), ((), ())), preferred_element_type=F32)

    def stage_state(i):
        st = st_s[...]
        for c in range(CPB):
            r0 = _item_rows(i) + c * CHUNK
            st = edec_s[pl.ds(r0, 1), :] * st + upd_s[i * CPB + c]
            oraw_s[pl.ds(r0, CHUNK), :] = lax.dot_general(
                q_ref[pl.ds(r0, CHUNK), :], st.astype(BF16), (((1,), (1,)), ((), ())),
                preferred_element_type=F32)
        st_s[...] = st

    def stage_out(i):
        rows = pl.ds(_item_rows(i), MIX_BLK)
        o = _rms(oraw_s[rows, :] * scale) * ng
        g = g_ref[rows, :].astype(F32)
        o_ref[rows, :] = (o * (g * _sig(g))).astype(BF16)

    _skewed_loop(seq // MIX_BLK, [stage_gate, stage_decay, stage_update, stage_state, stage_out])


def _gla(proj, small, wg_pad, b_gk, norm_g, bsz, seq):
    t = proj.shape[0]
    kq, kk = OFF_Q // GLA_HK, OFF_K // GLA_HK
    kv, kg = OFF_V // GLA_HV, OFF_G // GLA_HV
    return pl.pallas_call(
        _gla_kernel,
        grid=(bsz, GLA_HEADS),
        in_specs=[
            pl.BlockSpec((seq, GLA_HK), lambda b, h: (b, kq + h)),
            pl.BlockSpec((seq, GLA_HK), lambda b, h: (b, kk + h)),
            pl.BlockSpec((seq, GLA_HV), lambda b, h: (b, kv + h)),
            pl.BlockSpec((seq, GLA_HV), lambda b, h: (b, kg + h)),
            pl.BlockSpec((seq, LANES), lambda b, h: (b, 0)),
            pl.BlockSpec((LANES, GLA_HK), lambda b, h: (0, h)),
            pl.BlockSpec((1, GLA_HK), lambda b, h: (0, h)),
            pl.BlockSpec((1, GLA_HV), lambda b, h: (0, 0)),
        ],
        out_specs=pl.BlockSpec((seq, GLA_HV), lambda b, h: (b, h)),
        out_shape=jax.ShapeDtypeStruct((t, GLA_DV), BF16),
        scratch_shapes=[
            pltpu.VMEM((seq, GLA_HK), F32),
            pltpu.VMEM((seq, GLA_HK), BF16),
            pltpu.VMEM((seq, GLA_HK), F32),
            pltpu.VMEM((seq // CHUNK, GLA_HV, GLA_HK), F32),
            pltpu.VMEM((GLA_HV, GLA_HK), F32),
            pltpu.VMEM((seq, GLA_HV), F32),
        ],
        compiler_params=_cparams(("arbitrary", "arbitrary")),
        name="gla_mixer",
    )(proj, proj, proj, proj, small, wg_pad, b_gk, norm_g)


def _conv_halo(ref, r0, lanes):
    if isinstance(r0, int):
        if r0 == 0:
            return jnp.zeros((8, lanes.stop - lanes.start), F32)
        return ref[pl.ds(r0 - 16, 16), lanes].astype(F32)[8:, :]
    prev = ref[pl.ds(pl.multiple_of(jnp.maximum(r0 - 16, 0), 16), 16), lanes].astype(F32)[8:, :]
    return jnp.where(r0 > 0, prev, 0.0)


def _conv_silu(ref, r0, lanes, w_ref, b_ref):
    u = ref[pl.ds(r0, CONV_ROWS), lanes].astype(F32)
    win = jnp.concatenate([_conv_halo(ref, r0, lanes), u], axis=0)
    acc = u * w_ref[SSM_CONV - 1:SSM_CONV, lanes] + b_ref[:, lanes]
    for k in range(1, SSM_CONV):
        shifted = pltpu.roll(win, k, 0)[8:8 + CONV_ROWS, :]
        acc = acc + shifted * w_ref[SSM_CONV - 1 - k:SSM_CONV - k, lanes]
    return acc * _sig(acc)


def _ssd_kernel(z_ref, xs_ref, b_ref, c_ref, sm_ref, cwx_ref, cwb_ref, cwc_ref, cbx_ref, cbb_ref,
                cbc_ref, alog_ref, dsk_ref, ng_ref, o_ref,
                xs_s, b_s, c_s, dt_s, a_s, ea_s, edec_s, xdt_s, decx_s, m_s, upd_s, st_s, y_s):
    seq = z_ref.shape[0]
    grp = pl.program_id(1)
    er = lax.broadcasted_iota(jnp.int32, (LANES, SSM_GW), 0)
    el = lax.broadcasted_iota(jnp.int32, (LANES, SSM_GW), 1)
    expand = jnp.where(er == SM_DT + SSM_HPG * grp + el // SSM_HEADDIM, 1.0, 0.0).astype(BF16)
    scan = _chunk_scan_matrix(MIX_BLK)
    a_row = -jnp.exp(alog_ref[...])
    dsk = dsk_ref[...]
    ng = ng_ref[...]
    dr = lax.broadcasted_iota(jnp.int32, (CHUNK, SSM_GW), 0)
    dl = lax.broadcasted_iota(jnp.int32, (CHUNK, SSM_GW), 1)
    diag = jnp.where(dr == dl % CHUNK, 1.0, 0.0)
    br = lax.broadcasted_iota(jnp.int32, (SSM_GW, SSM_GW), 0)
    bl = lax.broadcasted_iota(jnp.int32, (SSM_GW, SSM_GW), 1)
    blockmask = jnp.where(br // SSM_HEADDIM == bl // SSM_HEADDIM, 1.0, 0.0).astype(BF16)
    st_s[...] = jnp.zeros_like(st_s)

    def stage_conv(i):
        rows = pl.ds(_item_rows(i), MIX_BLK)
        dt_s[rows, :] = _dot2_lhs(sm_ref[rows, :], expand)
        for sub in range(MIX_BLK // CONV_ROWS):
            r0 = _item_rows(i) + sub * CONV_ROWS
            out_rows = pl.ds(r0, CONV_ROWS)
            for strip in range(SSM_GW // LANES):
                lanes = slice(strip * LANES, (strip + 1) * LANES)
                xs_s[out_rows, lanes] = _conv_silu(xs_ref, r0, lanes, cwx_ref, cbx_ref)
            lanes = slice(0, SSM_DSTATE)
            b_s[out_rows, :] = _conv_silu(b_ref, r0, lanes, cwb_ref, cbb_ref).astype(BF16)
            c_s[out_rows, :] = _conv_silu(c_ref, r0, lanes, cwc_ref, cbc_ref).astype(BF16)

    def stage_decay(i):
        rows = pl.ds(_item_rows(i), MIX_BLK)
        dt = dt_s[rows, :]
        acum, atot = _chunk_scan(scan, dt * a_row)
        a_s[rows, :] = acum
        ea_s[rows, :] = jnp.exp(acum)
        edec_s[rows, :] = jnp.exp(atot)
        xdt = xs_s[rows, :] * dt
        xdt_s[rows, :] = xdt.astype(BF16)
        decx_s[rows, :] = (xdt * jnp.exp(atot - acum)).astype(BF16)

    def stage_intra(i):
        for c in range(CPB):
            rows = pl.ds(_item_rows(i) + c * CHUNK, CHUNK)
            acum = a_s[rows, :]
            acum_s = jnp.sum(acum * diag, axis=0, keepdims=True)
            lmask = jnp.exp(-jnp.abs(acum - acum_s))
            bm = b_s[rows, :]
            cb = lax.dot_general(c_s[rows, :], jnp.concatenate([bm] * SSM_HPG, axis=0),
                                 (((1,), (1,)), ((), ())), preferred_element_type=F32)
            m_s[rows, :] = (cb * lmask).astype(BF16)
            upd_s[i * CPB + c] = lax.dot_general(bm, decx_s[rows, :], (((0,), (0,)), ((), ())),
                                                 preferred_element_type=F32)

    def stage_state(i):
        st = st_s[...]
        for c in range(CPB):
            r0 = _item_rows(i) + c * CHUNK
            rows = pl.ds(r0, CHUNK)
            xdt_bd = jnp.concatenate([xdt_s[rows, :]] * SSM_HPG, axis=0) * blockmask
            y = _mm(m_s[rows, :], xdt_bd) + _mm(c_s[rows, :], st.astype(BF16)) * ea_s[rows, :]
            st = edec_s[pl.ds(r0, 1), :] * st + upd_s[i * CPB + c]
            y = y + xs_s[rows, :] * dsk
            z = z_ref[rows, :].astype(F32)
            y_s[rows, :] = y * (z * _sig(z))
        st_s[...] = st

    def stage_out(i):
        rows = pl.ds(_item_rows(i), MIX_BLK)
        o_ref[rows, :] = (_rms(y_s[rows, :]) * ng).astype(BF16)

    _skewed_loop(seq // MIX_BLK, [stage_conv, stage_decay, stage_intra, stage_state, stage_out])


def _ssd(proj, small, conv_w, conv_b, a_log_e, d_skip_e, norm_g, bsz, seq):
    t = proj.shape[0]
    kz, kx = OFF_Z // SSM_GW, OFF_XS // SSM_GW
    kb, kc = OFF_B // SSM_DSTATE, OFF_C // SSM_DSTATE
    nb = SSM_DINNER // SSM_DSTATE
    ncg = SSM_GROUPS
    wide_f32 = pltpu.VMEM((seq, SSM_GW), F32)
    wide_bf16 = pltpu.VMEM((seq, SSM_GW), BF16)
    return pl.pallas_call(
        _ssd_kernel,
        grid=(bsz, SSM_GROUPS),
        in_specs=[
            pl.BlockSpec((seq, SSM_GW), lambda b, g: (b, kz + g)),
            pl.BlockSpec((seq, SSM_GW), lambda b, g: (b, kx + g)),
            pl.BlockSpec((seq, SSM_DSTATE), lambda b, g: (b, kb + g)),
            pl.BlockSpec((seq, SSM_DSTATE), lambda b, g: (b, kc + g)),
            pl.BlockSpec((seq, LANES), lambda b, g: (b, 0)),
            pl.BlockSpec((SSM_CONV, SSM_GW), lambda b, g: (0, g)),
            pl.BlockSpec((SSM_CONV, SSM_DSTATE), lambda b, g: (0, nb + g)),
            pl.BlockSpec((SSM_CONV, SSM_DSTATE), lambda b, g: (0, nb + ncg + g)),
            pl.BlockSpec((1, SSM_GW), lambda b, g: (0, g)),
            pl.BlockSpec((1, SSM_DSTATE), lambda b, g: (0, nb + g)),
            pl.BlockSpec((1, SSM_DSTATE), lambda b, g: (0, nb + ncg + g)),
            pl.BlockSpec((1, SSM_GW), lambda b, g: (0, g)),
            pl.BlockSpec((1, SSM_GW), lambda b, g: (0, g)),
            pl.BlockSpec((1, SSM_GW), lambda b, g: (0, g)),
        ],
        out_specs=pl.BlockSpec((seq, SSM_GW), lambda b, g: (b, g)),
        out_shape=jax.ShapeDtypeStruct((t, SSM_DINNER), BF16),
        scratch_shapes=[
            wide_f32,
            pltpu.VMEM((seq, SSM_DSTATE), BF16),
            pltpu.VMEM((seq, SSM_DSTATE), BF16),
            wide_f32,
            wide_f32,
            wide_f32,
            wide_f32,
            wide_bf16,
            wide_bf16,
            wide_bf16,
            pltpu.VMEM((seq // CHUNK, SSM_DSTATE, SSM_GW), F32),
            pltpu.VMEM((SSM_DSTATE, SSM_GW), F32),
            wide_f32,
        ],
        compiler_params=_cparams(("arbitrary", "arbitrary")),
        name="ssd_mixer",
    )(proj, proj, proj, proj, small, conv_w, conv_w, conv_w, conv_b, conv_b, conv_b,
      a_log_e, d_skip_e, norm_g)


def _lane_min(x):
    return jnp.min(x, axis=-1, keepdims=True)


def _lane_max(x):
    return jnp.max(x, axis=-1, keepdims=True)


def _merge_kernel(oa_ref, ob_ref, ga_ref, gb_ref, x_ref, gt_ref, woa_ref, wob_ref, wout_ref,
                  g2_ref, sc_ref, sh_ref, wr_ref, br_ref,
                  x1_ref, h2_ref, ri_ref, rw_ref, cnt_ref, cnt_s):
    i = pl.program_id(0)
    tm = x_ref.shape[0]

    @pl.when(i == 0)
    def _():
        cnt_s[...] = jnp.zeros_like(cnt_s)

    ya = _mm(oa_ref[...], woa_ref[...])
    yb = _mm(ob_ref[...], wob_ref[...])
    merged = _sig(ga_ref[...].astype(F32)) * ya + _sig(gb_ref[...].astype(F32)) * yb
    x1 = x_ref[...] + gt_ref[0] * _mm(merged.astype(BF16), wout_ref[...])
    x1_ref[...] = x1
    h = _rms(x1) * g2_ref[...] * (1.0 + sc_ref[0]) + sh_ref[0]
    h2_ref[...] = h

    logit = _dot3(h, wr_ref[...]) + br_ref[...]
    lane = lax.broadcasted_iota(jnp.int32, (tm, LANES), 1)
    lane_f = lane.astype(F32)
    neg = jnp.float32(-jnp.inf)
    big = jnp.float32(LANES)
    is_g = lane < MOE_GROUPS
    lg = jnp.where(is_g, logit, neg)
    gmax = _lane_max(lg)
    gsum = jnp.sum(jnp.where(is_g, jnp.exp(lg - gmax), 0.0), axis=-1, keepdims=True)
    gidx = _lane_min(jnp.where(lg == gmax, lane_f, big))
    lo = RL_E + MOE_EPG * gidx
    in_grp = (lane_f >= lo) & (lane_f < lo + MOE_EPG)
    le = jnp.where(in_grp, logit, neg)
    emax = _lane_max(le)
    l0 = _lane_min(jnp.where(le == emax, lane_f, big))
    le2 = jnp.where(lane_f == l0, neg, le)
    emax2 = _lane_max(le2)
    l1 = _lane_min(jnp.where(le2 == emax2, lane_f, big))
    ratio = jnp.exp(emax2 - emax)
    w0 = 1.0 / (gsum * (1.0 + ratio))
    w1 = w0 * ratio

    oh0 = lane_f == l0
    oh1 = lane_f == l1
    oh = jnp.where(oh0 | oh1, 1.0, 0.0)
    tr = lax.broadcasted_iota(jnp.int32, (tm, tm), 0)
    tc = lax.broadcasted_iota(jnp.int32, (tm, tm), 1)
    before = _mm(jnp.where(tr > tc, 1.0, 0.0).astype(BF16), oh.astype(BF16)) + cnt_s[0:1, :]
    r0 = jnp.sum(jnp.where(oh0, before, 0.0), axis=-1, keepdims=True)
    r1 = jnp.sum(jnp.where(oh1, before, 0.0), axis=-1, keepdims=True)
    total = cnt_s[0:1, :] + jnp.sum(oh, axis=0, keepdims=True)
    cnt_s[...] = jnp.broadcast_to(total, cnt_s.shape)
    cnt_ref[...] = jnp.broadcast_to(total, cnt_ref.shape)

    ri = jnp.where(lane == 0, l0 - RL_E,
                   jnp.where(lane == 1, l1 - RL_E, jnp.where(lane == 2, r0, jnp.where(lane == 3, r1, 0.0))))
    ri_ref[...] = ri.astype(jnp.int32)
    rw_ref[...] = jnp.where(lane == 0, w0, jnp.where(lane == 1, w1, 0.0))


def _merge(o_a, o_b, proj, x, gt1, w_oa, w_ob, w_out, g2, sc2, sh2, w_r, b_r, seq):
    t, d = x.shape
    tm = TM_MERGE
    tpb = seq // tm
    kga, kgb = OFF_GA // d, OFF_GB // d
    const = lambda i: (0, 0)
    return pl.pallas_call(
        _merge_kernel,
        grid=(t // tm,),
        in_specs=[
            pl.BlockSpec((tm, GLA_DV), lambda i: (i, 0)),
            pl.BlockSpec((tm, SSM_DINNER), lambda i: (i, 0)),
            pl.BlockSpec((tm, d), lambda i: (i, kga)),
            pl.BlockSpec((tm, d), lambda i: (i, kgb)),
            pl.BlockSpec((tm, d), lambda i: (i, 0)),
            pl.BlockSpec((1, 1, d), lambda i: (i // tpb, 0, 0)),
            pl.BlockSpec((GLA_DV, d), const),
            pl.BlockSpec((SSM_DINNER, d), const),
            pl.BlockSpec((d, d), const),
            pl.BlockSpec((1, d), const),
            pl.BlockSpec((1, 1, d), lambda i: (i // tpb, 0, 0)),
            pl.BlockSpec((1, 1, d), lambda i: (i // tpb, 0, 0)),
            pl.BlockSpec((d, LANES), const),
            pl.BlockSpec((1, LANES), const),
        ],
        out_specs=[
            pl.BlockSpec((tm, d), lambda i: (i, 0)),
            pl.BlockSpec((tm, d), lambda i: (i, 0)),
            pl.BlockSpec((tm, LANES), lambda i: (i, 0)),
            pl.BlockSpec((tm, LANES), lambda i: (i, 0)),
            pl.BlockSpec((8, LANES), const),
        ],
        out_shape=[
            jax.ShapeDtypeStruct((t, d), F32),
            jax.ShapeDtypeStruct((t, d), F32),
            jax.ShapeDtypeStruct((t, LANES), jnp.int32),
            jax.ShapeDtypeStruct((t, LANES), F32),
            jax.ShapeDtypeStruct((8, LANES), F32),
        ],
        scratch_shapes=[pltpu.VMEM((8, LANES), F32)],
        compiler_params=_cparams(("arbitrary",)),
        name="merge_router",
    )(o_a, o_b, proj, proj, x, gt1, w_oa, w_ob, w_out, g2, sc2, sh2, w_r, b_r)


def _pos_kernel(ri_ref, st_ref, pos_ref):
    ri = ri_ref[...].astype(F32)
    lane = lax.broadcasted_iota(jnp.int32, ri.shape, 1)
    lane_f = lane.astype(F32)
    starts = st_ref[...]
    p0 = jnp.sum(jnp.where(lane_f == ri[:, 0:1], starts, 0.0), axis=-1, keepdims=True) + ri[:, 2:3]
    p1 = jnp.sum(jnp.where(lane_f == ri[:, 1:2], starts, 0.0), axis=-1, keepdims=True) + ri[:, 3:4]
    pos_ref[...] = jnp.where(lane == 0, p0, jnp.where(lane == 1, p1, 0.0)).astype(jnp.int32)


def _positions(ri, starts_lanes):
    t = ri.shape[0]
    return pl.pallas_call(
        _pos_kernel,
        grid=(t // TM_POS,),
        in_specs=[
            pl.BlockSpec((TM_POS, LANES), lambda i: (i, 0)),
            pl.BlockSpec((1, LANES), lambda i: (0, 0)),
        ],
        out_specs=pl.BlockSpec((TM_POS, LANES), lambda i: (i, 0)),
        out_shape=jax.ShapeDtypeStruct((t, LANES), jnp.int32),
        compiler_params=_cparams(("arbitrary",)),
        name="moe_positions",
    )(ri, starts_lanes)


def _row_copy(src_ref, src_row, dst_ref, dst_row, sem):
    return pltpu.make_async_copy(src_ref.at[pl.ds(src_row, 1)], dst_ref.at[pl.ds(dst_row, 1)], sem)


def _dispatch_kernel(pos_ref, h_ref, init_ref, xs_ref, sem):
    del init_ref
    n_tok = h_ref.shape[0]

    def issue(tok, carry):
        _row_copy(h_ref, tok, xs_ref, pos_ref[2 * tok], sem).start()
        _row_copy(h_ref, tok, xs_ref, pos_ref[2 * tok + 1], sem).start()
        return carry

    lax.fori_loop(0, n_tok, issue, 0, unroll=8)

    def drain(tok, carry):
        _row_copy(h_ref, 0, xs_ref, 0, sem).wait()
        _row_copy(h_ref, 0, xs_ref, 0, sem).wait()
        return carry

    lax.fori_loop(0, n_tok, drain, 0, unroll=8)


def _dispatch(pos_flat, h2, n_rows):
    t, d = h2.shape
    return pl.pallas_call(
        _dispatch_kernel,
        grid=(t // DISPATCH_BLK,),
        in_specs=[
            pl.BlockSpec((2 * DISPATCH_BLK,), lambda i: (i,), memory_space=pltpu.SMEM),
            pl.BlockSpec((DISPATCH_BLK, d), lambda i: (i, 0)),
            pl.BlockSpec(memory_space=pl.ANY),
        ],
        out_specs=pl.BlockSpec(memory_space=pl.ANY),
        out_shape=jax.ShapeDtypeStruct((n_rows, d), F32),
        scratch_shapes=[pltpu.SemaphoreType.DMA(())],
        input_output_aliases={2: 0},
        compiler_params=_cparams(("arbitrary",)),
        name="moe_dispatch",
    )(pos_flat, h2, jnp.zeros((n_rows, d), F32))


def _expert_kernel(te_ref, nv_ref, x_ref, w1_ref, w3_ref, w2_ref, o_ref, w13_s, w2_s):
    i = pl.program_id(0)
    valid = i < nv_ref[0]
    new_expert = (i == 0) | (te_ref[i] != te_ref[jnp.maximum(i - 1, 0)])

    @pl.when(valid & new_expert)
    def _():
        w13_s[:, :MOE_DFF] = w1_ref[0].astype(BF16)
        w13_s[:, MOE_DFF:] = w3_ref[0].astype(BF16)
        w2_s[...] = w2_ref[0].astype(BF16)

    @pl.when(valid)
    def _():
        ab = _mm(x_ref[...].astype(BF16), w13_s[...])
        a = ab[:, :MOE_DFF]
        o_ref[...] = _mm((a * _sig(a) * ab[:, MOE_DFF:]).astype(BF16), w2_s[...])

    @pl.when(jnp.logical_not(valid))
    def _():
        o_ref[...] = jnp.zeros_like(o_ref)


def _experts(tile_expert, n_valid, xs, w1, w3, w2):
    n_rows, d = xs.shape
    tm = TM_EXPERT
    grid_spec = pltpu.PrefetchScalarGridSpec(
        num_scalar_prefetch=2,
        grid=(n_rows // tm,),
        in_specs=[
            pl.BlockSpec((tm, d), lambda i, te, nv: (jnp.minimum(i, nv[0] - 1), 0)),
            pl.BlockSpec((1, d, MOE_DFF), lambda i, te, nv: (te[i], 0, 0)),
            pl.BlockSpec((1, d, MOE_DFF), lambda i, te, nv: (te[i], 0, 0)),
            pl.BlockSpec((1, MOE_DFF, d), lambda i, te, nv: (te[i], 0, 0)),
        ],
        out_specs=pl.BlockSpec((tm, d), lambda i, te, nv: (i, 0)),
        scratch_shapes=[pltpu.VMEM((d, 2 * MOE_DFF), BF16), pltpu.VMEM((MOE_DFF, d), BF16)],
    )
    return pl.pallas_call(
        _expert_kernel,
        grid_spec=grid_spec,
        out_shape=jax.ShapeDtypeStruct((n_rows, d), F32),
        compiler_params=_cparams(("arbitrary",)),
        name="moe_experts",
    )(tile_expert, n_valid, xs, w1, w3, w2)


def _combine_kernel(pos_ref, ys_ref, x_ref, rw_ref, gt_ref, fg_ref, o_ref, buf, sem, *, final):
    tm = x_ref.shape[0]

    def issue(tok, carry):
        _row_copy(ys_ref, pos_ref[2 * tok], buf.at[0], tok, sem).start()
        _row_copy(ys_ref, pos_ref[2 * tok + 1], buf.at[1], tok, sem).start()
        return carry

    lax.fori_loop(0, tm, issue, 0, unroll=8)

    def drain(tok, carry):
        _row_copy(ys_ref, 0, buf.at[0], 0, sem).wait()
        _row_copy(ys_ref, 0, buf.at[1], 0, sem).wait()
        return carry

    lax.fori_loop(0, tm, drain, 0, unroll=8)

    rw = rw_ref[...]
    y = rw[:, 0:1] * buf[0] + rw[:, 1:2] * buf[1]
    out = x_ref[...] + gt_ref[0] * y
    if final:
        out = _rms(out) * fg_ref[...]
    o_ref[...] = out


def _combine(pos_flat, ys, x1, rw, gt2, final_g, seq, final):
    t, d = x1.shape
    tm = TM_COMBINE
    tpb = seq // tm
    return pl.pallas_call(
        functools.partial(_combine_kernel, final=final),
        grid=(t // tm,),
        in_specs=[
            pl.BlockSpec((2 * tm,), lambda i: (i,), memory_space=pltpu.SMEM),
            pl.BlockSpec(memory_space=pl.ANY),
            pl.BlockSpec((tm, d), lambda i: (i, 0)),
            pl.BlockSpec((tm, LANES), lambda i: (i, 0)),
            pl.BlockSpec((1, 1, d), lambda i: (i // tpb, 0, 0)),
            pl.BlockSpec((1, d), lambda i: (0, 0)),
        ],
        out_specs=pl.BlockSpec((tm, d), lambda i: (i, 0)),
        out_shape=jax.ShapeDtypeStruct((t, d), F32),
        scratch_shapes=[pltpu.VMEM((2, tm, d), F32), pltpu.SemaphoreType.DMA(())],
        compiler_params=_cparams(("arbitrary",)),
        name="moe_combine",
    )(pos_flat, ys, x1, rw, gt2, final_g)


def _routing_tables(ri, cnt, n_tiles):
    counts = cnt[0, RL_E:RL_E + MOE_EXPERTS].astype(jnp.int32)
    padded = ((counts + TM_EXPERT - 1) // TM_EXPERT) * TM_EXPERT
    ends = jnp.cumsum(padded)
    starts = ends - padded
    starts_lanes = jnp.concatenate(
        [starts.astype(F32), jnp.zeros((LANES - MOE_EXPERTS,), F32)]).reshape(1, LANES)
    pos = _positions(ri, starts_lanes)[:, :2].reshape(-1)
    n_valid = (ends[-1] // TM_EXPERT).astype(jnp.int32).reshape(1)
    tile_start = jnp.arange(n_tiles, dtype=jnp.int32) * TM_EXPERT
    tile_expert = jnp.minimum(
        jnp.sum((tile_start[:, None] >= ends[None, :]).astype(jnp.int32), axis=1), MOE_EXPERTS - 1)
    return pos, tile_expert, n_valid


def kernel(x, c, ada_w, ada_b, norm1_g, w_in, gla_w_gk2, gla_b_gk, gla_norm_g, conv_w, conv_b,
           dt_bias, a_log, d_skip, ssm_norm_g, w_oa, w_ob, w_out, norm2_g, router_group_w,
           router_group_b, router_expert_w, router_expert_b, expert_w1, expert_w3, expert_w2,
           final_norm_g):
    bsz, seq, d = x.shape
    depth = ada_w.shape[0]
    t = bsz * seq
    n_tiles = (2 * t) // TM_EXPERT + MOE_EXPERTS
    n_rows = n_tiles * TM_EXPERT

    mod = _modulation(c, ada_w, ada_b)
    xf = x.reshape(t, d)
    c_gk, c_dt = OFF_G + GLA_DV, OFF_C + SSM_GROUPS * SSM_DSTATE + GLA_RANK
    for l in range(depth):
        sh1, sc1, gt1, sh2, sc2, gt2 = [mod[l, :, k * d:(k + 1) * d].reshape(bsz, 1, d) for k in range(N_MOD)]
        w = w_in[l]
        w_main = jnp.concatenate(
            [w[:, :c_gk], w[:, c_gk + GLA_RANK:c_dt], w[:, c_dt + SSM_HEADS:]], axis=1).astype(BF16)
        pad = jnp.zeros((d, LANES - GLA_RANK - SSM_HEADS), F32)
        w_small = jnp.concatenate([w[:, c_gk:c_gk + GLA_RANK], w[:, c_dt:c_dt + SSM_HEADS], pad], axis=1)
        dt_bias_pad = jnp.concatenate([jnp.zeros((GLA_RANK,), F32), dt_bias[l], pad[0]]).reshape(1, LANES)
        proj, small = _norm_proj(xf, norm1_g[l].reshape(1, d), sc1, sh1, w_main, w_small, dt_bias_pad, seq)

        wg_pad = jnp.concatenate([gla_w_gk2[l], jnp.zeros((LANES - GLA_RANK, GLA_DK), F32)], axis=0)
        o_a = _gla(proj, small, wg_pad, gla_b_gk[l].reshape(1, GLA_DK),
                   gla_norm_g[l].reshape(1, GLA_HV), bsz, seq)
        o_b = _ssd(proj, small, conv_w[l], conv_b[l].reshape(1, -1),
                   jnp.repeat(a_log[l], SSM_HEADDIM).reshape(1, SSM_DINNER),
                   jnp.repeat(d_skip[l], SSM_HEADDIM).reshape(1, SSM_DINNER),
                   ssm_norm_g[l].reshape(1, SSM_DINNER), bsz, seq)

        w_r = jnp.concatenate([router_group_w[l], router_expert_w[l],
                               jnp.zeros((d, LANES - MOE_GROUPS - MOE_EXPERTS), F32)], axis=1)
        b_r = jnp.concatenate([router_group_b[l], router_expert_b[l],
                               jnp.zeros((LANES - MOE_GROUPS - MOE_EXPERTS,), F32)]).reshape(1, LANES)
        x1, h2, ri, rw, cnt = _merge(o_a, o_b, proj, xf, gt1, w_oa[l].astype(BF16), w_ob[l].astype(BF16),
                                     w_out[l].astype(BF16), norm2_g[l].reshape(1, d), sc2, sh2, w_r, b_r, seq)

        pos, tile_expert, n_valid = _routing_tables(ri, cnt, n_tiles)
        xs = _dispatch(pos, h2, n_rows)
        ys = _experts(tile_expert, n_valid, xs,
                      expert_w1[l].reshape(MOE_EXPERTS, d, MOE_DFF),
                      expert_w3[l].reshape(MOE_EXPERTS, d, MOE_DFF),
                      expert_w2[l].reshape(MOE_EXPERTS, MOE_DFF, d))
        xf = _combine(pos, ys, x1, rw, gt2, final_norm_g.reshape(1, d), seq, final=(l == depth - 1))
    return xf.reshape(bsz, seq, d)
```

```python
import functools

import jax
import jax.numpy as jnp
from jax import lax
from jax.experimental import pallas as pl
from jax.experimental.pallas import tpu as pltpu

F32 = jnp.float32
BF16 = jnp.bfloat16

EPS = 1e-6
CHUNK = 64
D_MODEL = 1024
N_MOD = 6
GLA_HEADS = 4
GLA_HK = 128
GLA_HV = 256
GLA_DK = GLA_HEADS * GLA_HK
GLA_DV = GLA_HEADS * GLA_HV
GLA_RANK = 16
GLA_NORMALIZER = 16.0
SSM_DINNER = 2048
SSM_HEADDIM = 64
SSM_HEADS = SSM_DINNER // SSM_HEADDIM
SSM_GROUPS = 8
SSM_HPG = SSM_HEADS // SSM_GROUPS
SSM_GW = SSM_HPG * SSM_HEADDIM
SSM_DSTATE = 128
SSM_CONV = 4
MOE_GROUPS = 4
MOE_EPG = 8
MOE_EXPERTS = MOE_GROUPS * MOE_EPG
MOE_DFF = 256

OFF_Q = 0
OFF_K = OFF_Q + GLA_DK
OFF_V = OFF_K + GLA_DK
OFF_G = OFF_V + GLA_DV
OFF_Z = OFF_G + GLA_DV
OFF_XS = OFF_Z + SSM_DINNER
OFF_B = OFF_XS + SSM_DINNER
OFF_C = OFF_B + SSM_GROUPS * SSM_DSTATE
OFF_GA = OFF_C + SSM_GROUPS * SSM_DSTATE
OFF_GB = OFF_GA + D_MODEL
N_PROJ = OFF_GB + D_MODEL
LANES = 128
SM_DT = GLA_RANK
RL_E = MOE_GROUPS

V7X_VMEM_LIMIT = 56 * 1024 * 1024

TM_PROJ = 1024
NJ_PROJ = 4
TN_PROJ = N_PROJ // NJ_PROJ
TM_MERGE = 512
TM_EXPERT = 256
TM_COMBINE = 256
TM_POS = 2048
DISPATCH_BLK = 1024
MIX_BLK = 256
CPB = MIX_BLK // CHUNK
CONV_ROWS = 128
CONV_HALO = 16
CONV_WIN = CONV_HALO + CONV_ROWS


def _cparams(sem):
    return pltpu.CompilerParams(dimension_semantics=sem, vmem_limit_bytes=V7X_VMEM_LIMIT)


def _sig(x):
    return 0.5 * jnp.tanh(0.5 * x) + 0.5


def _softplus(x):
    return jnp.maximum(x, 0.0) + jnp.log(1.0 + jnp.exp(-jnp.abs(x)))


def _log_sigmoid(x):
    return jnp.minimum(x, 0.0) - jnp.log(1.0 + jnp.exp(-jnp.abs(x)))


def _split(a):
    hi = a.astype(BF16)
    lo = (a - hi.astype(F32)).astype(BF16)
    return hi, lo


def _mm(a, b):
    return jnp.dot(a, b, preferred_element_type=F32)


def _dot3(a, b):
    ah, al = _split(a)
    bh, bl = _split(b)
    return _mm(ah, bh) + _mm(al, bh) + _mm(ah, bl)


def _dot2_lhs(a, b_bf16):
    ah, al = _split(a)
    return _mm(ah, b_bf16) + _mm(al, b_bf16)


def _chunk_scan_matrix(n):
    r = lax.broadcasted_iota(jnp.int32, (2 * n, n), 0)
    c = lax.broadcasted_iota(jnp.int32, (2 * n, n), 1)
    t = jnp.where(r >= n, r - n, r)
    keep = (t // CHUNK == c // CHUNK) & ((r >= n) | (t >= c))
    return jnp.where(keep, 1.0, 0.0).astype(BF16)


def _chunk_scan(mat, x):
    n, w = x.shape
    hi, lo = _split(x)
    out = _mm(mat, jnp.concatenate([hi, lo], axis=1))
    s = out[:, :w] + out[:, w:]
    return s[:n], s[n:]


def _rms(x):
    return x * lax.rsqrt(jnp.mean(x * x, axis=-1, keepdims=True) + EPS)


def _skewed_loop(n, stages):
    depth = len(stages)

    def step(i, lo, hi):
        for k in reversed(range(lo, hi)):
            stages[k](i - k)

    for i in range(depth - 1):
        step(i, 0, i + 1)

    def body(i, carry):
        step(i, 0, depth)
        return carry

    lax.fori_loop(depth - 1, n, body, 0)
    for i in range(n, n + depth - 1):
        step(i, i - n + 1, depth)


def _item_rows(i):
    start = i * MIX_BLK
    return start if isinstance(start, int) else pl.multiple_of(start, MIX_BLK)


def _mod_kernel(c_ref, w_ref, b_ref, o_ref):
    c = c_ref[...]
    o_ref[0] = _dot3(c * _sig(c), w_ref[0]) + b_ref[0]


def _modulation(c, ada_w, ada_b):
    depth, d, n = ada_w.shape
    bsz = c.shape[0]
    tn = 1536
    return pl.pallas_call(
        _mod_kernel,
        grid=(depth, n // tn),
        in_specs=[
            pl.BlockSpec((bsz, d), lambda l, j: (0, 0)),
            pl.BlockSpec((1, d, tn), lambda l, j: (l, 0, j)),
            pl.BlockSpec((1, 1, tn), lambda l, j: (l, 0, j)),
        ],
        out_specs=pl.BlockSpec((1, bsz, tn), lambda l, j: (l, 0, j)),
        out_shape=jax.ShapeDtypeStruct((depth, bsz, n), F32),
        compiler_params=_cparams(("arbitrary", "arbitrary")),
        name="adaln_mod",
    )(c, ada_w, ada_b.reshape(depth, 1, n))


def _norm_proj_kernel(x0_ref, xn_ref, g_ref, sc0_ref, sh0_ref, scn_ref, shn_ref, w_ref, ws_ref, dtb_ref,
                      o_ref, os_ref, h_s, side_s):
    i = pl.program_id(0)
    j = pl.program_id(1)
    cur = i % 2
    rows_per_step = TM_PROJ // NJ_PROJ

    def norm_rows(x_ref, sc_ref, sh_ref, rows, slot):
        h = _rms(x_ref[rows, :]) * g_ref[...] * (1.0 + sc_ref[0]) + sh_ref[0]
        h_s[slot, rows, :] = h.astype(BF16)
        side = _dot3(h, ws_ref[...])
        lane = lax.broadcasted_iota(jnp.int32, side.shape, 1)
        is_dt = (lane >= SM_DT) & (lane < SM_DT + SSM_HEADS)
        side_s[slot, rows, :] = jnp.where(is_dt, _softplus(side + dtb_ref[...]), side)

    @pl.when((i == 0) & (j == 0))
    def _():
        for r in range(NJ_PROJ):
            norm_rows(x0_ref, sc0_ref, sh0_ref, pl.ds(r * rows_per_step, rows_per_step), 0)

    norm_rows(xn_ref, scn_ref, shn_ref,
              pl.ds(pl.multiple_of(j * rows_per_step, rows_per_step), rows_per_step), 1 - cur)

    @pl.when(j == 0)
    def _():
        os_ref[...] = side_s[cur]

    o_ref[...] = _mm(h_s[cur], w_ref[...]).astype(BF16)


def _norm_proj(x, g, sc, sh, w_main, w_small, dt_bias_pad, seq):
    t, d = x.shape
    tpb = seq // TM_PROJ
    last = t // TM_PROJ - 1
    nxt = lambda i: jnp.minimum(i + 1, last)
    return pl.pallas_call(
        _norm_proj_kernel,
        grid=(t // TM_PROJ, NJ_PROJ),
        in_specs=[
            pl.BlockSpec((TM_PROJ, d), lambda i, j: (0, 0)),
            pl.BlockSpec((TM_PROJ, d), lambda i, j: (nxt(i), 0)),
            pl.BlockSpec((1, d), lambda i, j: (0, 0)),
            pl.BlockSpec((1, 1, d), lambda i, j: (0, 0, 0)),
            pl.BlockSpec((1, 1, d), lambda i, j: (0, 0, 0)),
            pl.BlockSpec((1, 1, d), lambda i, j: (nxt(i) // tpb, 0, 0)),
            pl.BlockSpec((1, 1, d), lambda i, j: (nxt(i) // tpb, 0, 0)),
            pl.BlockSpec((d, TN_PROJ), lambda i, j: (0, j)),
            pl.BlockSpec((d, LANES), lambda i, j: (0, 0)),
            pl.BlockSpec((1, LANES), lambda i, j: (0, 0)),
        ],
        out_specs=[
            pl.BlockSpec((TM_PROJ, TN_PROJ), lambda i, j: (i, j)),
            pl.BlockSpec((TM_PROJ, LANES), lambda i, j: (i, 0)),
        ],
        out_shape=[
            jax.ShapeDtypeStruct((t, N_PROJ), BF16),
            jax.ShapeDtypeStruct((t, LANES), F32),
        ],
        scratch_shapes=[pltpu.VMEM((2, TM_PROJ, d), BF16), pltpu.VMEM((2, TM_PROJ, LANES), F32)],
        compiler_params=_cparams(("arbitrary", "arbitrary")),
        name="norm_in_proj",
    )(x, x, g, sc, sh, sc, sh, w_main, w_small, dt_bias_pad)


def _gla_kernel(q_ref, k_ref, v_ref, g_ref, sm_ref, wg_ref, bg_ref, ng_ref, o_ref,
                gk_s, kdec_s, edec_s, upd_s, st_s, oraw_s):
    seq = q_ref.shape[0]
    scan = _chunk_scan_matrix(MIX_BLK)
    wg_hi, wg_lo = _split(wg_ref[...])
    bg = bg_ref[...]
    ng = ng_ref[...]
    scale = GLA_HK ** -0.5
    st_s[...] = jnp.zeros_like(st_s)

    def stage_gate(i):
        rows = pl.ds(_item_rows(i), MIX_BLK)
        sm_hi, sm_lo = _split(sm_ref[rows, :])
        pre = _mm(sm_hi, wg_hi) + _mm(sm_lo, wg_hi) + _mm(sm_hi, wg_lo) + bg
        gk_s[rows, :] = _log_sigmoid(pre) * (1.0 / GLA_NORMALIZER)

    def stage_decay(i):
        rows = pl.ds(_item_rows(i), MIX_BLK)
        cum, tot = _chunk_scan(scan, gk_s[rows, :])
        kdec_s[rows, :] = (k_ref[rows, :].astype(F32) * jnp.exp(tot - cum)).astype(BF16)
        edec_s[rows, :] = jnp.exp(tot)

    def stage_update(i):
        for c in range(CPB):
            rows = pl.ds(_item_rows(i) + c * CHUNK, CHUNK)
            upd_s[i * CPB + c] = lax.dot_general(
                v_ref[rows, :], kdec_s[rows, :], (((0,), (0,)), ((), ())), preferred_element_type=F32)

    def stage_state(i):
        st = st_s[...]
        for c in range(CPB):
            r0 = _item_rows(i) + c * CHUNK
            st = edec_s[pl.ds(r0, 1), :] * st + upd_s[i * CPB + c]
            oraw_s[pl.ds(r0, CHUNK), :] = lax.dot_general(
                q_ref[pl.ds(r0, CHUNK), :], st.astype(BF16), (((1,), (1,)), ((), ())),
                preferred_element_type=F32)
        st_s[...] = st

    def stage_out(i):
        rows = pl.ds(_item_rows(i), MIX_BLK)
        o = _rms(oraw_s[rows, :] * scale) * ng
        g = g_ref[rows, :].astype(F32)
        o_ref[rows, :] = (o * (g * _sig(g))).astype(BF16)

    _skewed_loop(seq // MIX_BLK, [stage_gate, stage_decay, stage_update, stage_state, stage_out])


def _gla(proj, small, wg_pad, b_gk, norm_g, bsz, seq):
    t = proj.shape[0]
    kq, kk = OFF_Q // GLA_HK, OFF_K // GLA_HK
    kv, kg = OFF_V // GLA_HV, OFF_G // GLA_HV
    return pl.pallas_call(
        _gla_kernel,
        grid=(bsz, GLA_HEADS),
        in_specs=[
            pl.BlockSpec((seq, GLA_HK), lambda b, h: (b, kq + h)),
            pl.BlockSpec((seq, GLA_HK), lambda b, h: (b, kk + h)),
            pl.BlockSpec((seq, GLA_HV), lambda b, h: (b, kv + h)),
            pl.BlockSpec((seq, GLA_HV), lambda b, h: (b, kg + h)),
            pl.BlockSpec((seq, LANES), lambda b, h: (b, 0)),
            pl.BlockSpec((LANES, GLA_HK), lambda b, h: (0, h)),
            pl.BlockSpec((1, GLA_HK), lambda b, h: (0, h)),
            pl.BlockSpec((1, GLA_HV), lambda b, h: (0, 0)),
        ],
        out_specs=pl.BlockSpec((seq, GLA_HV), lambda b, h: (b, h)),
        out_shape=jax.ShapeDtypeStruct((t, GLA_DV), BF16),
        scratch_shapes=[
            pltpu.VMEM((seq, GLA_HK), F32),
            pltpu.VMEM((seq, GLA_HK), BF16),
            pltpu.VMEM((seq, GLA_HK), F32),
            pltpu.VMEM((seq // CHUNK, GLA_HV, GLA_HK), F32),
            pltpu.VMEM((GLA_HV, GLA_HK), F32),
            pltpu.VMEM((seq, GLA_HV), F32),
        ],
        compiler_params=_cparams(("arbitrary", "arbitrary")),
        name="gla_mixer",
    )(proj, proj, proj, proj, small, wg_pad, b_gk, norm_g)


def _conv_shift_matrix():
    t = lax.broadcasted_iota(jnp.int32, (CONV_ROWS, SSM_CONV * CONV_WIN), 0)
    c = lax.broadcasted_iota(jnp.int32, (CONV_ROWS, SSM_CONV * CONV_WIN), 1)
    return jnp.where(c % CONV_WIN == t + CONV_HALO - c // CONV_WIN, 1.0, 0.0).astype(BF16)


def _conv_window(ref, r0):
    if isinstance(r0, int) and r0 == 0:
        return jnp.concatenate([jnp.zeros((CONV_HALO, ref.shape[1]), BF16), ref[0:CONV_ROWS, :]], axis=0)
    start = r0 - CONV_HALO
    return ref[pl.ds(start if isinstance(start, int) else pl.multiple_of(start, CONV_HALO), CONV_WIN), :]


def _ssd_kernel(z_ref, xs_ref, b_ref, c_ref, sm_ref, cwx_ref, cwb_ref, cwc_ref, cbx_ref, cbb_ref,
                cbc_ref, alog_ref, dsk_ref, ng_ref, o_ref,
                xs_s, b_s, c_s, dt_s, a_s, ea_s, edec_s, xdt_s, decx_s, m_s, upd_s, st_s, y_s):
    seq = z_ref.shape[0]
    grp = pl.program_id(1)
    er = lax.broadcasted_iota(jnp.int32, (LANES, SSM_GW), 0)
    el = lax.broadcasted_iota(jnp.int32, (LANES, SSM_GW), 1)
    expand = jnp.where(er == SM_DT + SSM_HPG * grp + el // SSM_HEADDIM, 1.0, 0.0).astype(BF16)
    scan = _chunk_scan_matrix(MIX_BLK)
    a_row = -jnp.exp(alog_ref[...])
    dsk = dsk_ref[...]
    ng = ng_ref[...]
    dr = lax.broadcasted_iota(jnp.int32, (CHUNK, SSM_GW), 0)
    dl = lax.broadcasted_iota(jnp.int32, (CHUNK, SSM_GW), 1)
    diag = jnp.where(dr == dl % CHUNK, 1.0, 0.0)
    br = lax.broadcasted_iota(jnp.int32, (SSM_GW, SSM_GW), 0)
    bl = lax.broadcasted_iota(jnp.int32, (SSM_GW, SSM_GW), 1)
    blockmask = jnp.where(br // SSM_HEADDIM == bl // SSM_HEADDIM, 1.0, 0.0).astype(BF16)
    st_s[...] = jnp.zeros_like(st_s)
    shift = _conv_shift_matrix()
    taps = jnp.concatenate([cwx_ref[...], cwb_ref[...], cwc_ref[...]], axis=1)
    conv_bias = jnp.concatenate([cbx_ref[...], cbb_ref[...], cbc_ref[...]], axis=1)

    def stage_conv(i):
        rows = pl.ds(_item_rows(i), MIX_BLK)
        dt_s[rows, :] = _dot2_lhs(sm_ref[rows, :], expand)
        for sub in range(MIX_BLK // CONV_ROWS):
            r0 = _item_rows(i) + sub * CONV_ROWS
            win = jnp.concatenate(
                [_conv_window(xs_ref, r0), _conv_window(b_ref, r0), _conv_window(c_ref, r0)], axis=1)
            scaled = jnp.concatenate(
                [win * taps[SSM_CONV - 1 - k:SSM_CONV - k, :].astype(BF16) for k in range(SSM_CONV)], axis=0)
            acc = _mm(shift, scaled) + conv_bias
            act = acc * _sig(acc)
            out_rows = pl.ds(r0, CONV_ROWS)
            xs_s[out_rows, :] = act[:, :SSM_GW]
            b_s[out_rows, :] = act[:, SSM_GW:SSM_GW + SSM_DSTATE].astype(BF16)
            c_s[out_rows, :] = act[:, SSM_GW + SSM_DSTATE:].astype(BF16)

    def stage_decay(i):
        rows = pl.ds(_item_rows(i), MIX_BLK)
        dt = dt_s[rows, :]
        acum, atot = _chunk_scan(scan, dt * a_row)
        a_s[rows, :] = acum
        ea_s[rows, :] = jnp.exp(acum)
        edec_s[rows, :] = jnp.exp(atot)
        xdt = xs_s[rows, :] * dt
        xdt_s[rows, :] = xdt.astype(BF16)
        decx_s[rows, :] = (xdt * jnp.exp(atot - acum)).astype(BF16)

    def stage_intra(i):
        for c in range(CPB):
            rows = pl.ds(_item_rows(i) + c * CHUNK, CHUNK)
            acum = a_s[rows, :]
            acum_s = jnp.sum(acum * diag, axis=0, keepdims=True)
            lmask = jnp.exp(-jnp.abs(acum - acum_s))
            bm = b_s[rows, :]
            cb = lax.dot_general(c_s[rows, :], jnp.concatenate([bm] * SSM_HPG, axis=0),
                                 (((1,), (1,)), ((), ())), preferred_element_type=F32)
            m_s[rows, :] = (cb * lmask).astype(BF16)
            upd_s[i * CPB + c] = lax.dot_general(bm, decx_s[rows, :], (((0,), (0,)), ((), ())),
                                                 preferred_element_type=F32)

    def stage_state(i):
        st = st_s[...]
        for c in range(CPB):
            r0 = _item_rows(i) + c * CHUNK
            rows = pl.ds(r0, CHUNK)
            xdt_bd = jnp.concatenate([xdt_s[rows, :]] * SSM_HPG, axis=0) * blockmask
            y = _mm(m_s[rows, :], xdt_bd) + _mm(c_s[rows, :], st.astype(BF16)) * ea_s[rows, :]
            st = edec_s[pl.ds(r0, 1), :] * st + upd_s[i * CPB + c]
            y = y + xs_s[rows, :] * dsk
            z = z_ref[rows, :].astype(F32)
            y_s[rows, :] = y * (z * _sig(z))
        st_s[...] = st

    def stage_out(i):
        rows = pl.ds(_item_rows(i), MIX_BLK)
        o_ref[rows, :] = (_rms(y_s[rows, :]) * ng).astype(BF16)

    _skewed_loop(seq // MIX_BLK, [stage_conv, stage_decay, stage_intra, stage_state, stage_out])


def _ssd(proj, small, conv_w, conv_b, a_log_e, d_skip_e, norm_g, bsz, seq):
    t = proj.shape[0]
    kz, kx = OFF_Z // SSM_GW, OFF_XS // SSM_GW
    kb, kc = OFF_B // SSM_DSTATE, OFF_C // SSM_DSTATE
    nb = SSM_DINNER // SSM_DSTATE
    ncg = SSM_GROUPS
    wide_f32 = pltpu.VMEM((seq, SSM_GW), F32)
    wide_bf16 = pltpu.VMEM((seq, SSM_GW), BF16)
    return pl.pallas_call(
        _ssd_kernel,
        grid=(bsz, SSM_GROUPS),
        in_specs=[
            pl.BlockSpec((seq, SSM_GW), lambda b, g: (b, kz + g)),
            pl.BlockSpec((seq, SSM_GW), lambda b, g: (b, kx + g)),
            pl.BlockSpec((seq, SSM_DSTATE), lambda b, g: (b, kb + g)),
            pl.BlockSpec((seq, SSM_DSTATE), lambda b, g: (b, kc + g)),
            pl.BlockSpec((seq, LANES), lambda b, g: (b, 0)),
            pl.BlockSpec((SSM_CONV, SSM_GW), lambda b, g: (0, g)),
            pl.BlockSpec((SSM_CONV, SSM_DSTATE), lambda b, g: (0, nb + g)),
            pl.BlockSpec((SSM_CONV, SSM_DSTATE), lambda b, g: (0, nb + ncg + g)),
            pl.BlockSpec((1, SSM_GW), lambda b, g: (0, g)),
            pl.BlockSpec((1, SSM_DSTATE), lambda b, g: (0, nb + g)),
            pl.BlockSpec((1, SSM_DSTATE), lambda b, g: (0, nb + ncg + g)),
            pl.BlockSpec((1, SSM_GW), lambda b, g: (0, g)),
            pl.BlockSpec((1, SSM_GW), lambda b, g: (0, g)),
            pl.BlockSpec((1, SSM_GW), lambda b, g: (0, g)),
        ],
        out_specs=pl.BlockSpec((seq, SSM_GW), lambda b, g: (b, g)),
        out_shape=jax.ShapeDtypeStruct((t, SSM_DINNER), BF16),
        scratch_shapes=[
            wide_f32,
            pltpu.VMEM((seq, SSM_DSTATE), BF16),
            pltpu.VMEM((seq, SSM_DSTATE), BF16),
            wide_f32,
            wide_f32,
            wide_f32,
            wide_f32,
            wide_bf16,
            wide_bf16,
            wide_bf16,
            pltpu.VMEM((seq // CHUNK, SSM_DSTATE, SSM_GW), F32),
            pltpu.VMEM((SSM_DSTATE, SSM_GW), F32),
            wide_f32,
        ],
        compiler_params=_cparams(("arbitrary", "arbitrary")),
        name="ssd_mixer",
    )(proj, proj, proj, proj, small, conv_w, conv_w, conv_w, conv_b, conv_b, conv_b,
      a_log_e, d_skip_e, norm_g)


def _lane_min(x):
    return jnp.min(x, axis=-1, keepdims=True)


def _lane_max(x):
    return jnp.max(x, axis=-1, keepdims=True)


def _merge_kernel(oa_ref, ob_ref, ga_ref, gb_ref, x_ref, gt_ref, woa_ref, wob_ref, wout_ref,
                  g2_ref, sc_ref, sh_ref, wr_ref, br_ref,
                  x1_ref, h2_ref, ri_ref, rw_ref, cnt_ref, cnt_s):
    i = pl.program_id(0)
    tm = x_ref.shape[0]

    @pl.when(i == 0)
    def _():
        cnt_s[...] = jnp.zeros_like(cnt_s)

    ya = _mm(oa_ref[...], woa_ref[...])
    yb = _mm(ob_ref[...], wob_ref[...])
    merged = _sig(ga_ref[...].astype(F32)) * ya + _sig(gb_ref[...].astype(F32)) * yb
    x1 = x_ref[...] + gt_ref[0] * _mm(merged.astype(BF16), wout_ref[...])
    x1_ref[...] = x1
    h = _rms(x1) * g2_ref[...] * (1.0 + sc_ref[0]) + sh_ref[0]
    h2_ref[...] = h

    logit = _dot3(h, wr_ref[...]) + br_ref[...]
    lane = lax.broadcasted_iota(jnp.int32, (tm, LANES), 1)
    lane_f = lane.astype(F32)
    neg = jnp.float32(-jnp.inf)
    big = jnp.float32(LANES)
    is_g = lane < MOE_GROUPS
    lg = jnp.where(is_g, logit, neg)
    gmax = _lane_max(lg)
    gsum = jnp.sum(jnp.where(is_g, jnp.exp(lg - gmax), 0.0), axis=-1, keepdims=True)
    gidx = _lane_min(jnp.where(lg == gmax, lane_f, big))
    lo = RL_E + MOE_EPG * gidx
    in_grp = (lane_f >= lo) & (lane_f < lo + MOE_EPG)
    le = jnp.where(in_grp, logit, neg)
    emax = _lane_max(le)
    l0 = _lane_min(jnp.where(le == emax, lane_f, big))
    le2 = jnp.where(lane_f == l0, neg, le)
    emax2 = _lane_max(le2)
    l1 = _lane_min(jnp.where(le2 == emax2, lane_f, big))
    ratio = jnp.exp(emax2 - emax)
    w0 = 1.0 / (gsum * (1.0 + ratio))
    w1 = w0 * ratio

    oh0 = lane_f == l0
    oh1 = lane_f == l1
    oh = jnp.where(oh0 | oh1, 1.0, 0.0)
    tr = lax.broadcasted_iota(jnp.int32, (tm, tm), 0)
    tc = lax.broadcasted_iota(jnp.int32, (tm, tm), 1)
    before = _mm(jnp.where(tr > tc, 1.0, 0.0).astype(BF16), oh.astype(BF16)) + cnt_s[0:1, :]
    r0 = jnp.sum(jnp.where(oh0, before, 0.0), axis=-1, keepdims=True)
    r1 = jnp.sum(jnp.where(oh1, before, 0.0), axis=-1, keepdims=True)
    total = cnt_s[0:1, :] + jnp.sum(oh, axis=0, keepdims=True)
    cnt_s[...] = jnp.broadcast_to(total, cnt_s.shape)
    cnt_ref[...] = jnp.broadcast_to(total, cnt_ref.shape)

    ri = jnp.where(lane == 0, l0 - RL_E,
                   jnp.where(lane == 1, l1 - RL_E, jnp.where(lane == 2, r0, jnp.where(lane == 3, r1, 0.0))))
    ri_ref[...] = ri.astype(jnp.int32)
    rw_ref[...] = jnp.where(lane == 0, w0, jnp.where(lane == 1, w1, 0.0))


def _merge(o_a, o_b, proj, x, gt1, w_oa, w_ob, w_out, g2, sc2, sh2, w_r, b_r, seq):
    t, d = x.shape
    tm = TM_MERGE
    tpb = seq // tm
    kga, kgb = OFF_GA // d, OFF_GB // d
    const = lambda i: (0, 0)
    return pl.pallas_call(
        _merge_kernel,
        grid=(t // tm,),
        in_specs=[
            pl.BlockSpec((tm, GLA_DV), lambda i: (i, 0)),
            pl.BlockSpec((tm, SSM_DINNER), lambda i: (i, 0)),
            pl.BlockSpec((tm, d), lambda i: (i, kga)),
            pl.BlockSpec((tm, d), lambda i: (i, kgb)),
            pl.BlockSpec((tm, d), lambda i: (i, 0)),
            pl.BlockSpec((1, 1, d), lambda i: (i // tpb, 0, 0)),
            pl.BlockSpec((GLA_DV, d), const),
            pl.BlockSpec((SSM_DINNER, d), const),
            pl.BlockSpec((d, d), const),
            pl.BlockSpec((1, d), const),
            pl.BlockSpec((1, 1, d), lambda i: (i // tpb, 0, 0)),
            pl.BlockSpec((1, 1, d), lambda i: (i // tpb, 0, 0)),
            pl.BlockSpec((d, LANES), const),
            pl.BlockSpec((1, LANES), const),
        ],
        out_specs=[
            pl.BlockSpec((tm, d), lambda i: (i, 0)),
            pl.BlockSpec((tm, d), lambda i: (i, 0)),
            pl.BlockSpec((tm, LANES), lambda i: (i, 0)),
            pl.BlockSpec((tm, LANES), lambda i: (i, 0)),
            pl.BlockSpec((8, LANES), const),
        ],
        out_shape=[
            jax.ShapeDtypeStruct((t, d), F32),
            jax.ShapeDtypeStruct((t, d), F32),
            jax.ShapeDtypeStruct((t, LANES), jnp.int32),
            jax.ShapeDtypeStruct((t, LANES), F32),
            jax.ShapeDtypeStruct((8, LANES), F32),
        ],
        scratch_shapes=[pltpu.VMEM((8, LANES), F32)],
        compiler_params=_cparams(("arbitrary",)),
        name="merge_router",
    )(o_a, o_b, proj, proj, x, gt1, w_oa, w_ob, w_out, g2, sc2, sh2, w_r, b_r)


def _pos_kernel(ri_ref, st_ref, pos_ref):
    ri = ri_ref[...].astype(F32)
    lane = lax.broadcasted_iota(jnp.int32, ri.shape, 1)
    lane_f = lane.astype(F32)
    starts = st_ref[...]
    p0 = jnp.sum(jnp.where(lane_f == ri[:, 0:1], starts, 0.0), axis=-1, keepdims=True) + ri[:, 2:3]
    p1 = jnp.sum(jnp.where(lane_f == ri[:, 1:2], starts, 0.0), axis=-1, keepdims=True) + ri[:, 3:4]
    pos_ref[...] = jnp.where(lane == 0, p0, jnp.where(lane == 1, p1, 0.0)).astype(jnp.int32)


def _positions(ri, starts_lanes):
    t = ri.shape[0]
    return pl.pallas_call(
        _pos_kernel,
        grid=(t // TM_POS,),
        in_specs=[
            pl.BlockSpec((TM_POS, LANES), lambda i: (i, 0)),
            pl.BlockSpec((1, LANES), lambda i: (0, 0)),
        ],
        out_specs=pl.BlockSpec((TM_POS, LANES), lambda i: (i, 0)),
        out_shape=jax.ShapeDtypeStruct((t, LANES), jnp.int32),
        compiler_params=_cparams(("arbitrary",)),
        name="moe_positions",
    )(ri, starts_lanes)


def _row_copy(src_ref, src_row, dst_ref, dst_row, sem):
    return pltpu.make_async_copy(src_ref.at[pl.ds(src_row, 1)], dst_ref.at[pl.ds(dst_row, 1)], sem)


def _dispatch_kernel(pos_ref, ends_ref, h_ref, xs_ref, zero_s, sem, zsem):
    n_tok = h_ref.shape[0]

    @pl.when(pl.program_id(0) == 0)
    def _():
        zero_s[...] = jnp.zeros_like(zero_s)

        def last_tile(e):
            end = ends_ref[e]
            begin = jnp.where(e > 0, ends_ref[jnp.maximum(e - 1, 0)], 0)
            dst = xs_ref.at[pl.ds(pl.multiple_of(jnp.maximum(end - TM_EXPERT, 0), TM_EXPERT), TM_EXPERT)]
            return end > begin, pltpu.make_async_copy(zero_s, dst, zsem)

        def fill(e, carry):
            nonempty, cp = last_tile(e)

            @pl.when(nonempty)
            def _():
                cp.start()

            return carry

        def fill_done(e, carry):
            nonempty, cp = last_tile(e)

            @pl.when(nonempty)
            def _():
                cp.wait()

            return carry

        n_used = ends_ref[MOE_EXPERTS - 1] // TM_EXPERT

        def unused_tile(tile):
            dst = xs_ref.at[pl.ds(pl.multiple_of(tile * TM_EXPERT, TM_EXPERT), TM_EXPERT)]
            return pltpu.make_async_copy(zero_s, dst, zsem)

        def tail(tile, carry):
            unused_tile(tile).start()
            return carry

        def tail_done(tile, carry):
            unused_tile(tile).wait()
            return carry

        n_tiles = xs_ref.shape[0] // TM_EXPERT
        lax.fori_loop(0, MOE_EXPERTS, fill, 0)
        lax.fori_loop(n_used, n_tiles, tail, 0)
        lax.fori_loop(0, MOE_EXPERTS, fill_done, 0)
        lax.fori_loop(n_used, n_tiles, tail_done, 0)

    def issue(tok, carry):
        _row_copy(h_ref, tok, xs_ref, pos_ref[2 * tok], sem).start()
        _row_copy(h_ref, tok, xs_ref, pos_ref[2 * tok + 1], sem).start()
        return carry

    lax.fori_loop(0, n_tok, issue, 0, unroll=8)

    def drain(tok, carry):
        _row_copy(h_ref, 0, xs_ref, 0, sem).wait()
        _row_copy(h_ref, 0, xs_ref, 0, sem).wait()
        return carry

    lax.fori_loop(0, n_tok, drain, 0, unroll=8)


def _dispatch(pos_flat, ends, h2, n_rows):
    t, d = h2.shape
    return pl.pallas_call(
        _dispatch_kernel,
        grid=(t // DISPATCH_BLK,),
        in_specs=[
            pl.BlockSpec((2 * DISPATCH_BLK,), lambda i: (i,), memory_space=pltpu.SMEM),
            pl.BlockSpec(memory_space=pltpu.SMEM),
            pl.BlockSpec((DISPATCH_BLK, d), lambda i: (i, 0)),
        ],
        out_specs=pl.BlockSpec(memory_space=pl.ANY),
        out_shape=jax.ShapeDtypeStruct((n_rows, d), F32),
        scratch_shapes=[pltpu.VMEM((TM_EXPERT, d), F32), pltpu.SemaphoreType.DMA(()),
                        pltpu.SemaphoreType.DMA(())],
        compiler_params=_cparams(("arbitrary",)),
        name="moe_dispatch",
    )(pos_flat, ends, h2)


def _expert_kernel(te_ref, nv_ref, x_ref, w1_ref, w3_ref, w2_ref, o_ref, w13_s, w2_s):
    i = pl.program_id(0)
    valid = i < nv_ref[0]
    new_expert = (i == 0) | (te_ref[i] != te_ref[jnp.maximum(i - 1, 0)])

    @pl.when(valid & new_expert)
    def _():
        w13_s[:, :MOE_DFF] = w1_ref[0].astype(BF16)
        w13_s[:, MOE_DFF:] = w3_ref[0].astype(BF16)
        w2_s[...] = w2_ref[0].astype(BF16)

    @pl.when(valid)
    def _():
        ab = _mm(x_ref[...].astype(BF16), w13_s[...])
        a = ab[:, :MOE_DFF]
        o_ref[...] = _mm((a * _sig(a) * ab[:, MOE_DFF:]).astype(BF16), w2_s[...])

    @pl.when(jnp.logical_not(valid))
    def _():
        o_ref[...] = jnp.zeros_like(o_ref)


def _experts(tile_expert, n_valid, xs, w1, w3, w2, layer):
    n_rows, d = xs.shape
    tm = TM_EXPERT
    first = layer * MOE_EXPERTS
    grid_spec = pltpu.PrefetchScalarGridSpec(
        num_scalar_prefetch=2,
        grid=(n_rows // tm,),
        in_specs=[
            pl.BlockSpec((tm, d), lambda i, te, nv: (jnp.minimum(i, nv[0] - 1), 0)),
            pl.BlockSpec((1, d, MOE_DFF), lambda i, te, nv: (first + te[i], 0, 0)),
            pl.BlockSpec((1, d, MOE_DFF), lambda i, te, nv: (first + te[i], 0, 0)),
            pl.BlockSpec((1, MOE_DFF, d), lambda i, te, nv: (first + te[i], 0, 0)),
        ],
        out_specs=pl.BlockSpec((tm, d), lambda i, te, nv: (i, 0)),
        scratch_shapes=[pltpu.VMEM((d, 2 * MOE_DFF), BF16), pltpu.VMEM((MOE_DFF, d), BF16)],
    )
    return pl.pallas_call(
        _expert_kernel,
        grid_spec=grid_spec,
        out_shape=jax.ShapeDtypeStruct((n_rows, d), F32),
        compiler_params=_cparams(("arbitrary",)),
        name="moe_experts",
    )(tile_expert, n_valid, xs, w1, w3, w2)


def _combine_kernel(pos_ref, pos_next_ref, ys_ref, x_ref, rw_ref, gt_ref, fg_ref, o_ref, buf, sem, *, final):
    tm = x_ref.shape[0]
    i = pl.program_id(0)
    slot = i % 2

    def gather(p_ref, s):
        def issue(tok, carry):
            _row_copy(ys_ref, p_ref[2 * tok], buf.at[s, 0], tok, sem.at[s]).start()
            _row_copy(ys_ref, p_ref[2 * tok + 1], buf.at[s, 1], tok, sem.at[s]).start()
            return carry

        lax.fori_loop(0, tm, issue, 0, unroll=8)

    @pl.when(i == 0)
    def _():
        gather(pos_ref, 0)

    @pl.when(i + 1 < pl.num_programs(0))
    def _():
        gather(pos_next_ref, 1 - slot)

    def drain(tok, carry):
        _row_copy(ys_ref, 0, buf.at[slot, 0], 0, sem.at[slot]).wait()
        _row_copy(ys_ref, 0, buf.at[slot, 1], 0, sem.at[slot]).wait()
        return carry

    lax.fori_loop(0, tm, drain, 0, unroll=8)

    rw = rw_ref[...]
    y = rw[:, 0:1] * buf[slot, 0] + rw[:, 1:2] * buf[slot, 1]
    out = x_ref[...] + gt_ref[0] * y
    if final:
        out = _rms(out) * fg_ref[...]
    o_ref[...] = out


def _combine(pos_flat, ys, x1, rw, gt2, final_g, seq, final):
    t, d = x1.shape
    tm = TM_COMBINE
    tpb = seq // tm
    last = t // tm - 1
    return pl.pallas_call(
        functools.partial(_combine_kernel, final=final),
        grid=(t // tm,),
        in_specs=[
            pl.BlockSpec((2 * tm,), lambda i: (i,), memory_space=pltpu.SMEM),
            pl.BlockSpec((2 * tm,), lambda i: (jnp.minimum(i + 1, last),), memory_space=pltpu.SMEM),
            pl.BlockSpec(memory_space=pl.ANY),
            pl.BlockSpec((tm, d), lambda i: (i, 0)),
            pl.BlockSpec((tm, LANES), lambda i: (i, 0)),
            pl.BlockSpec((1, 1, d), lambda i: (i // tpb, 0, 0)),
            pl.BlockSpec((1, d), lambda i: (0, 0)),
        ],
        out_specs=pl.BlockSpec((tm, d), lambda i: (i, 0)),
        out_shape=jax.ShapeDtypeStruct((t, d), F32),
        scratch_shapes=[pltpu.VMEM((2, 2, tm, d), F32), pltpu.SemaphoreType.DMA((2,))],
        compiler_params=_cparams(("arbitrary",)),
        name="moe_combine",
    )(pos_flat, pos_flat, ys, x1, rw, gt2, final_g)


def _routing_tables(ri, cnt, n_tiles):
    counts = cnt[0, RL_E:RL_E + MOE_EXPERTS].astype(jnp.int32)
    padded = ((counts + TM_EXPERT - 1) // TM_EXPERT) * TM_EXPERT
    ends = jnp.cumsum(padded)
    starts = ends - padded
    starts_lanes = jnp.concatenate(
        [starts.astype(F32), jnp.zeros((LANES - MOE_EXPERTS,), F32)]).reshape(1, LANES)
    pos = _positions(ri, starts_lanes)[:, :2].reshape(-1)
    n_valid = (ends[-1] // TM_EXPERT).astype(jnp.int32).reshape(1)
    tile_start = jnp.arange(n_tiles, dtype=jnp.int32) * TM_EXPERT
    tile_expert = jnp.minimum(
        jnp.sum((tile_start[:, None] >= ends[None, :]).astype(jnp.int32), axis=1), MOE_EXPERTS - 1)
    return pos, ends, tile_expert, n_valid


def kernel(x, c, ada_w, ada_b, norm1_g, w_in, gla_w_gk2, gla_b_gk, gla_norm_g, conv_w, conv_b,
           dt_bias, a_log, d_skip, ssm_norm_g, w_oa, w_ob, w_out, norm2_g, router_group_w,
           router_group_b, router_expert_w, router_expert_b, expert_w1, expert_w3, expert_w2,
           final_norm_g):
    bsz, seq, d = x.shape
    depth = ada_w.shape[0]
    t = bsz * seq
    n_tiles = (2 * t) // TM_EXPERT + MOE_EXPERTS
    n_rows = n_tiles * TM_EXPERT

    mod = _modulation(c, ada_w, ada_b)
    xf = x.reshape(t, d)
    c_gk, c_dt = OFF_G + GLA_DV, OFF_C + SSM_GROUPS * SSM_DSTATE + GLA_RANK
    for l in range(depth):
        sh1, sc1, gt1, sh2, sc2, gt2 = [mod[l, :, k * d:(k + 1) * d].reshape(bsz, 1, d) for k in range(N_MOD)]
        w = w_in[l]
        w_main = jnp.concatenate(
            [w[:, :c_gk], w[:, c_gk + GLA_RANK:c_dt], w[:, c_dt + SSM_HEADS:]], axis=1).astype(BF16)
        pad = jnp.zeros((d, LANES - GLA_RANK - SSM_HEADS), F32)
        w_small = jnp.concatenate([w[:, c_gk:c_gk + GLA_RANK], w[:, c_dt:c_dt + SSM_HEADS], pad], axis=1)
        dt_bias_pad = jnp.concatenate([jnp.zeros((GLA_RANK,), F32), dt_bias[l], pad[0]]).reshape(1, LANES)
        proj, small = _norm_proj(xf, norm1_g[l].reshape(1, d), sc1, sh1, w_main, w_small, dt_bias_pad, seq)

        wg_pad = jnp.concatenate([gla_w_gk2[l], jnp.zeros((LANES - GLA_RANK, GLA_DK), F32)], axis=0)
        o_a = _gla(proj, small, wg_pad, gla_b_gk[l].reshape(1, GLA_DK),
                   gla_norm_g[l].reshape(1, GLA_HV), bsz, seq)
        o_b = _ssd(proj, small, conv_w[l], conv_b[l].reshape(1, -1),
                   jnp.repeat(a_log[l], SSM_HEADDIM).reshape(1, SSM_DINNER),
                   jnp.repeat(d_skip[l], SSM_HEADDIM).reshape(1, SSM_DINNER),
                   ssm_norm_g[l].reshape(1, SSM_DINNER), bsz, seq)

        w_r = jnp.concatenate([router_group_w[l], router_expert_w[l],
                               jnp.zeros((d, LANES - MOE_GROUPS - MOE_EXPERTS), F32)], axis=1)
        b_r = jnp.concatenate([router_group_b[l], router_expert_b[l],
                               jnp.zeros((LANES - MOE_GROUPS - MOE_EXPERTS,), F32)]).reshape(1, LANES)
        x1, h2, ri, rw, cnt = _merge(o_a, o_b, proj, xf, gt1, w_oa[l].astype(BF16), w_ob[l].astype(BF16),
                                     w_out[l].astype(BF16), norm2_g[l].reshape(1, d), sc2, sh2, w_r, b_r, seq)

        pos, ends, tile_expert, n_valid = _routing_tables(ri, cnt, n_tiles)
        xs = _dispatch(pos, ends, h2, n_rows)
        ys = _experts(tile_expert, n_valid, xs,
                      expert_w1.reshape(depth * MOE_EXPERTS, d, MOE_DFF),
                      expert_w3.reshape(depth * MOE_EXPERTS, d, MOE_DFF),
                      expert_w2.reshape(depth * MOE_EXPERTS, MOE_DFF, d), l)
        xf = _combine(pos, ys, x1, rw, gt2, final_norm_g.reshape(1, d), seq, final=(l == depth - 1))
    return xf.reshape(bsz, seq, d)
```

```python
import functools

import jax
import jax.numpy as jnp
from jax import lax
from jax.experimental import pallas as pl
from jax.experimental.pallas import tpu as pltpu

F32 = jnp.float32
BF16 = jnp.bfloat16

EPS = 1e-6
CHUNK = 64
D_MODEL = 1024
N_MOD = 6
GLA_HEADS = 4
GLA_HK = 128
GLA_HV = 256
GLA_DK = GLA_HEADS * GLA_HK
GLA_DV = GLA_HEADS * GLA_HV
GLA_RANK = 16
GLA_NORMALIZER = 16.0
SSM_DINNER = 2048
SSM_HEADDIM = 64
SSM_HEADS = SSM_DINNER // SSM_HEADDIM
SSM_GROUPS = 8
SSM_HPG = SSM_HEADS // SSM_GROUPS
SSM_GW = SSM_HPG * SSM_HEADDIM
SSM_DSTATE = 128
SSM_CONV = 4
MOE_GROUPS = 4
MOE_EPG = 8
MOE_EXPERTS = MOE_GROUPS * MOE_EPG
MOE_DFF = 256

OFF_Q = 0
OFF_K = OFF_Q + GLA_DK
OFF_V = OFF_K + GLA_DK
OFF_G = OFF_V + GLA_DV
OFF_Z = OFF_G + GLA_DV
OFF_XS = OFF_Z + SSM_DINNER
OFF_B = OFF_XS + SSM_DINNER
OFF_C = OFF_B + SSM_GROUPS * SSM_DSTATE
OFF_GA = OFF_C + SSM_GROUPS * SSM_DSTATE
OFF_GB = OFF_GA + D_MODEL
N_PROJ = OFF_GB + D_MODEL
LANES = 128
SM_DT = GLA_RANK
RL_E = MOE_GROUPS

V7X_VMEM_LIMIT = 56 * 1024 * 1024

TM_PROJ = 1024
NJ_PROJ = 4
TN_PROJ = N_PROJ // NJ_PROJ
TM_MERGE = 512
TM_EXPERT = 256
TM_COMBINE = 256
TM_POS = 2048
DISPATCH_BLK = 1024
MIX_BLK = 256
CPB = MIX_BLK // CHUNK
CONV_ROWS = 128
CONV_HALO = 16
CONV_WIN = CONV_HALO + CONV_ROWS


def _cparams(sem):
    return pltpu.CompilerParams(dimension_semantics=sem, vmem_limit_bytes=V7X_VMEM_LIMIT)


def _sig(x):
    return 0.5 * jnp.tanh(0.5 * x) + 0.5


def _softplus(x):
    return jnp.maximum(x, 0.0) + jnp.log(1.0 + jnp.exp(-jnp.abs(x)))


def _log_sigmoid(x):
    return jnp.minimum(x, 0.0) - jnp.log(1.0 + jnp.exp(-jnp.abs(x)))


def _split(a):
    hi = a.astype(BF16)
    lo = (a - hi.astype(F32)).astype(BF16)
    return hi, lo


def _mm(a, b):
    return jnp.dot(a, b, preferred_element_type=F32)


def _dot3(a, b):
    ah, al = _split(a)
    bh, bl = _split(b)
    return _mm(ah, bh) + _mm(al, bh) + _mm(ah, bl)


def _dot2_lhs(a, b_bf16):
    ah, al = _split(a)
    return _mm(ah, b_bf16) + _mm(al, b_bf16)


def _chunk_scan_matrix(n):
    r = lax.broadcasted_iota(jnp.int32, (2 * n, n), 0)
    c = lax.broadcasted_iota(jnp.int32, (2 * n, n), 1)
    t = jnp.where(r >= n, r - n, r)
    keep = (t // CHUNK == c // CHUNK) & ((r >= n) | (t >= c))
    return jnp.where(keep, 1.0, 0.0).astype(BF16)


def _chunk_scan(mat, x):
    n, w = x.shape
    hi, lo = _split(x)
    out = _mm(mat, jnp.concatenate([hi, lo], axis=1))
    s = out[:, :w] + out[:, w:]
    return s[:n], s[n:]


def _rms(x):
    return x * lax.rsqrt(jnp.mean(x * x, axis=-1, keepdims=True) + EPS)


def _pack_halves(x):
    w = x.shape[1] // 2
    lo = lax.bitcast_convert_type(x[:, :w].astype(BF16).astype(F32), jnp.uint32)
    hi = lax.bitcast_convert_type(x[:, w:].astype(BF16).astype(F32), jnp.uint32)
    return (lo >> 16) | (hi & jnp.uint32(0xFFFF0000))


def _unpack_halves(p):
    lo = lax.bitcast_convert_type(p << 16, F32)
    hi = lax.bitcast_convert_type(p & jnp.uint32(0xFFFF0000), F32)
    return lo, hi


def _skewed_loop(n, stages):
    depth = len(stages)

    def step(i, lo, hi):
        for k in reversed(range(lo, hi)):
            stages[k](i - k)

    for i in range(depth - 1):
        step(i, 0, i + 1)

    def body(i, carry):
        step(i, 0, depth)
        return carry

    lax.fori_loop(depth - 1, n, body, 0)
    for i in range(n, n + depth - 1):
        step(i, i - n + 1, depth)


def _item_rows(i):
    start = i * MIX_BLK
    return start if isinstance(start, int) else pl.multiple_of(start, MIX_BLK)


def _mod_kernel(c_ref, w_ref, b_ref, o_ref):
    c = c_ref[...]
    o_ref[0] = _dot3(c * _sig(c), w_ref[0]) + b_ref[0]


def _modulation(c, ada_w, ada_b):
    depth, d, n = ada_w.shape
    bsz = c.shape[0]
    tn = 1536
    return pl.pallas_call(
        _mod_kernel,
        grid=(depth, n // tn),
        in_specs=[
            pl.BlockSpec((bsz, d), lambda l, j: (0, 0)),
            pl.BlockSpec((1, d, tn), lambda l, j: (l, 0, j)),
            pl.BlockSpec((1, 1, tn), lambda l, j: (l, 0, j)),
        ],
        out_specs=pl.BlockSpec((1, bsz, tn), lambda l, j: (l, 0, j)),
        out_shape=jax.ShapeDtypeStruct((depth, bsz, n), F32),
        compiler_params=_cparams(("arbitrary", "arbitrary")),
        name="adaln_mod",
    )(c, ada_w, ada_b.reshape(depth, 1, n))


def _norm_proj_kernel(x0_ref, xn_ref, g_ref, sc0_ref, sh0_ref, scn_ref, shn_ref, w_ref, ws_ref, dtb_ref,
                      o_ref, os_ref, h_s, side_s):
    i = pl.program_id(0)
    j = pl.program_id(1)
    cur = i % 2
    rows_per_step = TM_PROJ // NJ_PROJ

    def norm_rows(x_ref, sc_ref, sh_ref, rows, slot):
        h = _rms(x_ref[rows, :]) * g_ref[...] * (1.0 + sc_ref[0]) + sh_ref[0]
        h_s[slot, rows, :] = h.astype(BF16)
        side = _dot3(h, ws_ref[...])
        lane = lax.broadcasted_iota(jnp.int32, side.shape, 1)
        is_dt = (lane >= SM_DT) & (lane < SM_DT + SSM_HEADS)
        side_s[slot, rows, :] = jnp.where(is_dt, _softplus(side + dtb_ref[...]), side)

    @pl.when((i == 0) & (j == 0))
    def _():
        for r in range(NJ_PROJ):
            norm_rows(x0_ref, sc0_ref, sh0_ref, pl.ds(r * rows_per_step, rows_per_step), 0)

    norm_rows(xn_ref, scn_ref, shn_ref,
              pl.ds(pl.multiple_of(j * rows_per_step, rows_per_step), rows_per_step), 1 - cur)

    @pl.when(j == 0)
    def _():
        os_ref[...] = side_s[cur]

    o_ref[...] = _mm(h_s[cur], w_ref[...]).astype(BF16)


def _norm_proj(x, g, sc, sh, w_main, w_small, dt_bias_pad, seq):
    t, d = x.shape
    tpb = seq // TM_PROJ
    last = t // TM_PROJ - 1
    nxt = lambda i: jnp.minimum(i + 1, last)
    return pl.pallas_call(
        _norm_proj_kernel,
        grid=(t // TM_PROJ, NJ_PROJ),
        in_specs=[
            pl.BlockSpec((TM_PROJ, d), lambda i, j: (0, 0)),
            pl.BlockSpec((TM_PROJ, d), lambda i, j: (nxt(i), 0)),
            pl.BlockSpec((1, d), lambda i, j: (0, 0)),
            pl.BlockSpec((1, 1, d), lambda i, j: (0, 0, 0)),
            pl.BlockSpec((1, 1, d), lambda i, j: (0, 0, 0)),
            pl.BlockSpec((1, 1, d), lambda i, j: (nxt(i) // tpb, 0, 0)),
            pl.BlockSpec((1, 1, d), lambda i, j: (nxt(i) // tpb, 0, 0)),
            pl.BlockSpec((d, TN_PROJ), lambda i, j: (0, j)),
            pl.BlockSpec((d, LANES), lambda i, j: (0, 0)),
            pl.BlockSpec((1, LANES), lambda i, j: (0, 0)),
        ],
        out_specs=[
            pl.BlockSpec((TM_PROJ, TN_PROJ), lambda i, j: (i, j)),
            pl.BlockSpec((TM_PROJ, LANES), lambda i, j: (i, 0)),
        ],
        out_shape=[
            jax.ShapeDtypeStruct((t, N_PROJ), BF16),
            jax.ShapeDtypeStruct((t, LANES), F32),
        ],
        scratch_shapes=[pltpu.VMEM((2, TM_PROJ, d), BF16), pltpu.VMEM((2, TM_PROJ, LANES), F32)],
        compiler_params=_cparams(("arbitrary", "arbitrary")),
        name="norm_in_proj",
    )(x, x, g, sc, sh, sc, sh, w_main, w_small, dt_bias_pad)


def _gla_kernel(q_ref, k_ref, v_ref, g_ref, sm_ref, wg_ref, bg_ref, ng_ref, o_ref,
                gk_s, kdec_s, edec_s, upd_s, st_s, oraw_s):
    seq = q_ref.shape[0]
    scan = _chunk_scan_matrix(MIX_BLK)
    wg_hi, wg_lo = _split(wg_ref[...])
    bg = bg_ref[...]
    ng = ng_ref[...]
    scale = GLA_HK ** -0.5
    st_s[...] = jnp.zeros_like(st_s)

    def stage_gate(i):
        rows = pl.ds(_item_rows(i), MIX_BLK)
        sm_hi, sm_lo = _split(sm_ref[rows, :])
        pre = _mm(sm_hi, wg_hi) + _mm(sm_lo, wg_hi) + _mm(sm_hi, wg_lo) + bg
        gk_s[rows, :] = _log_sigmoid(pre) * (1.0 / GLA_NORMALIZER)

    def stage_decay(i):
        rows = pl.ds(_item_rows(i), MIX_BLK)
        cum, tot = _chunk_scan(scan, gk_s[rows, :])
        kdec_s[rows, :] = (k_ref[rows, :].astype(F32) * jnp.exp(tot - cum)).astype(BF16)
        edec_s[rows, :] = jnp.exp(tot)

    def stage_update(i):
        for c in range(CPB):
            rows = pl.ds(_item_rows(i) + c * CHUNK, CHUNK)
            upd_s[i * CPB + c] = lax.dot_general(
                v_ref[rows, :], kdec_s[rows, :], (((0,), (0,)), ((), ())), preferred_element_type=F32)

    def stage_state(i):
        st = st_s[...]
        for c in range(CPB):
            r0 = _item_rows(i) + c * CHUNK
            st = edec_s[pl.ds(r0, 1), :] * st + upd_s[i * CPB + c]
            oraw_s[pl.ds(r0, CHUNK), :] = lax.dot_general(
                q_ref[pl.ds(r0, CHUNK), :], st.astype(BF16), (((1,), (1,)), ((), ())),
                preferred_element_type=F32)
        st_s[...] = st

    def stage_out(i):
        rows = pl.ds(_item_rows(i), MIX_BLK)
        o = _rms(oraw_s[rows, :] * scale) * ng
        g = g_ref[rows, :].astype(F32)
        o_ref[rows, :] = (o * (g * _sig(g))).astype(BF16)

    _skewed_loop(seq // MIX_BLK, [stage_gate, stage_decay, stage_update, stage_state, stage_out])


def _gla(proj, small, wg_pad, b_gk, norm_g, bsz, seq):
    t = proj.shape[0]
    kq, kk = OFF_Q // GLA_HK, OFF_K // GLA_HK
    kv, kg = OFF_V // GLA_HV, OFF_G // GLA_HV
    return pl.pallas_call(
        _gla_kernel,
        grid=(bsz, GLA_HEADS),
        in_specs=[
            pl.BlockSpec((seq, GLA_HK), lambda b, h: (b, kq + h)),
            pl.BlockSpec((seq, GLA_HK), lambda b, h: (b, kk + h)),
            pl.BlockSpec((seq, GLA_HV), lambda b, h: (b, kv + h)),
            pl.BlockSpec((seq, GLA_HV), lambda b, h: (b, kg + h)),
            pl.BlockSpec((seq, LANES), lambda b, h: (b, 0)),
            pl.BlockSpec((LANES, GLA_HK), lambda b, h: (0, h)),
            pl.BlockSpec((1, GLA_HK), lambda b, h: (0, h)),
            pl.BlockSpec((1, GLA_HV), lambda b, h: (0, 0)),
        ],
        out_specs=pl.BlockSpec((seq, GLA_HV), lambda b, h: (b, h)),
        out_shape=jax.ShapeDtypeStruct((t, GLA_DV), BF16),
        scratch_shapes=[
            pltpu.VMEM((seq, GLA_HK), F32),
            pltpu.VMEM((seq, GLA_HK), BF16),
            pltpu.VMEM((seq, GLA_HK), F32),
            pltpu.VMEM((seq // CHUNK, GLA_HV, GLA_HK), F32),
            pltpu.VMEM((GLA_HV, GLA_HK), F32),
            pltpu.VMEM((seq, GLA_HV), F32),
        ],
        compiler_params=_cparams(("arbitrary", "arbitrary")),
        name="gla_mixer",
    )(proj, proj, proj, proj, small, wg_pad, b_gk, norm_g)


def _conv_shift_matrix():
    t = lax.broadcasted_iota(jnp.int32, (CONV_ROWS, SSM_CONV * CONV_WIN), 0)
    c = lax.broadcasted_iota(jnp.int32, (CONV_ROWS, SSM_CONV * CONV_WIN), 1)
    return jnp.where(c % CONV_WIN == t + CONV_HALO - c // CONV_WIN, 1.0, 0.0).astype(BF16)


def _conv_window(ref, r0):
    if isinstance(r0, int) and r0 == 0:
        return jnp.concatenate([jnp.zeros((CONV_HALO, ref.shape[1]), BF16), ref[0:CONV_ROWS, :]], axis=0)
    start = r0 - CONV_HALO
    return ref[pl.ds(start if isinstance(start, int) else pl.multiple_of(start, CONV_HALO), CONV_WIN), :]


def _ssd_kernel(z_ref, xs_ref, b_ref, c_ref, sm_ref, cwx_ref, cwb_ref, cwc_ref, cbx_ref, cbb_ref,
                cbc_ref, alog_ref, dsk_ref, ng_ref, o_ref,
                xs_s, b_s, c_s, dt_s, a_s, ea_s, edec_s, xdt_s, decx_s, m_s, upd_s, st_s, y_s):
    seq = z_ref.shape[0]
    grp = pl.program_id(1)
    er = lax.broadcasted_iota(jnp.int32, (LANES, SSM_GW), 0)
    el = lax.broadcasted_iota(jnp.int32, (LANES, SSM_GW), 1)
    expand = jnp.where(er == SM_DT + SSM_HPG * grp + el // SSM_HEADDIM, 1.0, 0.0).astype(BF16)
    scan = _chunk_scan_matrix(MIX_BLK)
    a_row = -jnp.exp(alog_ref[...])
    dsk = dsk_ref[...]
    ng = ng_ref[...]
    dr = lax.broadcasted_iota(jnp.int32, (CHUNK, SSM_GW), 0)
    dl = lax.broadcasted_iota(jnp.int32, (CHUNK, SSM_GW), 1)
    diag = jnp.where(dr == dl % CHUNK, 1.0, 0.0)
    br = lax.broadcasted_iota(jnp.int32, (SSM_GW, SSM_GW), 0)
    bl = lax.broadcasted_iota(jnp.int32, (SSM_GW, SSM_GW), 1)
    blockmask = jnp.where(br // SSM_HEADDIM == bl // SSM_HEADDIM, 1.0, 0.0).astype(BF16)
    st_s[...] = jnp.zeros_like(st_s)
    shift = _conv_shift_matrix()
    taps = jnp.concatenate([cwx_ref[...], cwb_ref[...], cwc_ref[...]], axis=1)
    conv_bias = jnp.concatenate([cbx_ref[...], cbb_ref[...], cbc_ref[...]], axis=1)

    def stage_conv(i):
        rows = pl.ds(_item_rows(i), MIX_BLK)
        dt_s[rows, :] = _dot2_lhs(sm_ref[rows, :], expand)
        for sub in range(MIX_BLK // CONV_ROWS):
            r0 = _item_rows(i) + sub * CONV_ROWS
            win = jnp.concatenate(
                [_conv_window(xs_ref, r0), _conv_window(b_ref, r0), _conv_window(c_ref, r0)], axis=1)
            scaled = jnp.concatenate(
                [win * taps[SSM_CONV - 1 - k:SSM_CONV - k, :].astype(BF16) for k in range(SSM_CONV)], axis=0)
            acc = _mm(shift, scaled) + conv_bias
            act = acc * _sig(acc)
            out_rows = pl.ds(r0, CONV_ROWS)
            xs_s[out_rows, :] = act[:, :SSM_GW]
            b_s[out_rows, :] = act[:, SSM_GW:SSM_GW + SSM_DSTATE].astype(BF16)
            c_s[out_rows, :] = act[:, SSM_GW + SSM_DSTATE:].astype(BF16)

    def stage_decay(i):
        rows = pl.ds(_item_rows(i), MIX_BLK)
        dt = dt_s[rows, :]
        acum, atot = _chunk_scan(scan, dt * a_row)
        a_s[rows, :] = acum
        ea_s[rows, :] = jnp.exp(acum)
        edec_s[rows, :] = jnp.exp(atot)
        xdt = xs_s[rows, :] * dt
        xdt_s[rows, :] = xdt.astype(BF16)
        decx_s[rows, :] = (xdt * jnp.exp(atot - acum)).astype(BF16)

    def stage_intra(i):
        for c in range(CPB):
            rows = pl.ds(_item_rows(i) + c * CHUNK, CHUNK)
            acum = a_s[rows, :]
            acum_s = jnp.sum(acum * diag, axis=0, keepdims=True)
            lmask = jnp.exp(-jnp.abs(acum - acum_s))
            bm = b_s[rows, :]
            cb = lax.dot_general(c_s[rows, :], jnp.concatenate([bm] * SSM_HPG, axis=0),
                                 (((1,), (1,)), ((), ())), preferred_element_type=F32)
            m_s[rows, :] = (cb * lmask).astype(BF16)
            upd_s[i * CPB + c] = lax.dot_general(bm, decx_s[rows, :], (((0,), (0,)), ((), ())),
                                                 preferred_element_type=F32)

    def stage_state(i):
        st = st_s[...]
        for c in range(CPB):
            r0 = _item_rows(i) + c * CHUNK
            rows = pl.ds(r0, CHUNK)
            xdt_bd = jnp.concatenate([xdt_s[rows, :]] * SSM_HPG, axis=0) * blockmask
            y = _mm(m_s[rows, :], xdt_bd) + _mm(c_s[rows, :], st.astype(BF16)) * ea_s[rows, :]
            st = edec_s[pl.ds(r0, 1), :] * st + upd_s[i * CPB + c]
            y = y + xs_s[rows, :] * dsk
            z = z_ref[rows, :].astype(F32)
            y_s[rows, :] = y * (z * _sig(z))
        st_s[...] = st

    def stage_out(i):
        rows = pl.ds(_item_rows(i), MIX_BLK)
        o_ref[rows, :] = (_rms(y_s[rows, :]) * ng).astype(BF16)

    _skewed_loop(seq // MIX_BLK, [stage_conv, stage_decay, stage_intra, stage_state, stage_out])


def _ssd(proj, small, conv_w, conv_b, a_log_e, d_skip_e, norm_g, bsz, seq):
    t = proj.shape[0]
    kz, kx = OFF_Z // SSM_GW, OFF_XS // SSM_GW
    kb, kc = OFF_B // SSM_DSTATE, OFF_C // SSM_DSTATE
    nb = SSM_DINNER // SSM_DSTATE
    ncg = SSM_GROUPS
    wide_f32 = pltpu.VMEM((seq, SSM_GW), F32)
    wide_bf16 = pltpu.VMEM((seq, SSM_GW), BF16)
    return pl.pallas_call(
        _ssd_kernel,
        grid=(bsz, SSM_GROUPS),
        in_specs=[
            pl.BlockSpec((seq, SSM_GW), lambda b, g: (b, kz + g)),
            pl.BlockSpec((seq, SSM_GW), lambda b, g: (b, kx + g)),
            pl.BlockSpec((seq, SSM_DSTATE), lambda b, g: (b, kb + g)),
            pl.BlockSpec((seq, SSM_DSTATE), lambda b, g: (b, kc + g)),
            pl.BlockSpec((seq, LANES), lambda b, g: (b, 0)),
            pl.BlockSpec((SSM_CONV, SSM_GW), lambda b, g: (0, g)),
            pl.BlockSpec((SSM_CONV, SSM_DSTATE), lambda b, g: (0, nb + g)),
            pl.BlockSpec((SSM_CONV, SSM_DSTATE), lambda b, g: (0, nb + ncg + g)),
            pl.BlockSpec((1, SSM_GW), lambda b, g: (0, g)),
            pl.BlockSpec((1, SSM_DSTATE), lambda b, g: (0, nb + g)),
            pl.BlockSpec((1, SSM_DSTATE), lambda b, g: (0, nb + ncg + g)),
            pl.BlockSpec((1, SSM_GW), lambda b, g: (0, g)),
            pl.BlockSpec((1, SSM_GW), lambda b, g: (0, g)),
            pl.BlockSpec((1, SSM_GW), lambda b, g: (0, g)),
        ],
        out_specs=pl.BlockSpec((seq, SSM_GW), lambda b, g: (b, g)),
        out_shape=jax.ShapeDtypeStruct((t, SSM_DINNER), BF16),
        scratch_shapes=[
            wide_f32,
            pltpu.VMEM((seq, SSM_DSTATE), BF16),
            pltpu.VMEM((seq, SSM_DSTATE), BF16),
            wide_f32,
            wide_f32,
            wide_f32,
            wide_f32,
            wide_bf16,
            wide_bf16,
            wide_bf16,
            pltpu.VMEM((seq // CHUNK, SSM_DSTATE, SSM_GW), F32),
            pltpu.VMEM((SSM_DSTATE, SSM_GW), F32),
            wide_f32,
        ],
        compiler_params=_cparams(("arbitrary", "arbitrary")),
        name="ssd_mixer",
    )(proj, proj, proj, proj, small, conv_w, conv_w, conv_w, conv_b, conv_b, conv_b,
      a_log_e, d_skip_e, norm_g)


def _lane_min(x):
    return jnp.min(x, axis=-1, keepdims=True)


def _lane_max(x):
    return jnp.max(x, axis=-1, keepdims=True)


def _merge_kernel(oa_ref, ob_ref, ga_ref, gb_ref, x_ref, gt_ref, woa_ref, wob_ref, wout_ref,
                  g2_ref, sc_ref, sh_ref, wr_ref, br_ref,
                  x1_ref, h2_ref, ri_ref, rw_ref, cnt_ref, cnt_s):
    i = pl.program_id(0)
    tm = x_ref.shape[0]

    @pl.when(i == 0)
    def _():
        cnt_s[...] = jnp.zeros_like(cnt_s)

    ya = _mm(oa_ref[...], woa_ref[...])
    yb = _mm(ob_ref[...], wob_ref[...])
    merged = _sig(ga_ref[...].astype(F32)) * ya + _sig(gb_ref[...].astype(F32)) * yb
    x1 = x_ref[...] + gt_ref[0] * _mm(merged.astype(BF16), wout_ref[...])
    x1_ref[...] = x1
    h = _rms(x1) * g2_ref[...] * (1.0 + sc_ref[0]) + sh_ref[0]
    h2_ref[...] = _pack_halves(h)

    logit = _dot3(h, wr_ref[...]) + br_ref[...]
    lane = lax.broadcasted_iota(jnp.int32, (tm, LANES), 1)
    lane_f = lane.astype(F32)
    neg = jnp.float32(-jnp.inf)
    big = jnp.float32(LANES)
    is_g = lane < MOE_GROUPS
    lg = jnp.where(is_g, logit, neg)
    gmax = _lane_max(lg)
    gsum = jnp.sum(jnp.where(is_g, jnp.exp(lg - gmax), 0.0), axis=-1, keepdims=True)
    gidx = _lane_min(jnp.where(lg == gmax, lane_f, big))
    lo = RL_E + MOE_EPG * gidx
    in_grp = (lane_f >= lo) & (lane_f < lo + MOE_EPG)
    le = jnp.where(in_grp, logit, neg)
    emax = _lane_max(le)
    l0 = _lane_min(jnp.where(le == emax, lane_f, big))
    le2 = jnp.where(lane_f == l0, neg, le)
    emax2 = _lane_max(le2)
    l1 = _lane_min(jnp.where(le2 == emax2, lane_f, big))
    ratio = jnp.exp(emax2 - emax)
    w0 = 1.0 / (gsum * (1.0 + ratio))
    w1 = w0 * ratio

    oh0 = lane_f == l0
    oh1 = lane_f == l1
    oh = jnp.where(oh0 | oh1, 1.0, 0.0)
    tr = lax.broadcasted_iota(jnp.int32, (tm, tm), 0)
    tc = lax.broadcasted_iota(jnp.int32, (tm, tm), 1)
    before = _mm(jnp.where(tr > tc, 1.0, 0.0).astype(BF16), oh.astype(BF16)) + cnt_s[0:1, :]
    r0 = jnp.sum(jnp.where(oh0, before, 0.0), axis=-1, keepdims=True)
    r1 = jnp.sum(jnp.where(oh1, before, 0.0), axis=-1, keepdims=True)
    total = cnt_s[0:1, :] + jnp.sum(oh, axis=0, keepdims=True)
    cnt_s[...] = jnp.broadcast_to(total, cnt_s.shape)
    cnt_ref[...] = jnp.broadcast_to(total, cnt_ref.shape)

    ri = jnp.where(lane == 0, l0 - RL_E,
                   jnp.where(lane == 1, l1 - RL_E, jnp.where(lane == 2, r0, jnp.where(lane == 3, r1, 0.0))))
    ri_ref[...] = ri.astype(jnp.int32)
    rw_ref[...] = jnp.where(lane == 0, w0, jnp.where(lane == 1, w1, 0.0))


def _merge(o_a, o_b, proj, x, gt1, w_oa, w_ob, w_out, g2, sc2, sh2, w_r, b_r, seq):
    t, d = x.shape
    tm = TM_MERGE
    tpb = seq // tm
    kga, kgb = OFF_GA // d, OFF_GB // d
    const = lambda i: (0, 0)
    return pl.pallas_call(
        _merge_kernel,
        grid=(t // tm,),
        in_specs=[
            pl.BlockSpec((tm, GLA_DV), lambda i: (i, 0)),
            pl.BlockSpec((tm, SSM_DINNER), lambda i: (i, 0)),
            pl.BlockSpec((tm, d), lambda i: (i, kga)),
            pl.BlockSpec((tm, d), lambda i: (i, kgb)),
            pl.BlockSpec((tm, d), lambda i: (i, 0)),
            pl.BlockSpec((1, 1, d), lambda i: (i // tpb, 0, 0)),
            pl.BlockSpec((GLA_DV, d), const),
            pl.BlockSpec((SSM_DINNER, d), const),
            pl.BlockSpec((d, d), const),
            pl.BlockSpec((1, d), const),
            pl.BlockSpec((1, 1, d), lambda i: (i // tpb, 0, 0)),
            pl.BlockSpec((1, 1, d), lambda i: (i // tpb, 0, 0)),
            pl.BlockSpec((d, LANES), const),
            pl.BlockSpec((1, LANES), const),
        ],
        out_specs=[
            pl.BlockSpec((tm, d), lambda i: (i, 0)),
            pl.BlockSpec((tm, d // 2), lambda i: (i, 0)),
            pl.BlockSpec((tm, LANES), lambda i: (i, 0)),
            pl.BlockSpec((tm, LANES), lambda i: (i, 0)),
            pl.BlockSpec((8, LANES), const),
        ],
        out_shape=[
            jax.ShapeDtypeStruct((t, d), F32),
            jax.ShapeDtypeStruct((t, d // 2), jnp.uint32),
            jax.ShapeDtypeStruct((t, LANES), jnp.int32),
            jax.ShapeDtypeStruct((t, LANES), F32),
            jax.ShapeDtypeStruct((8, LANES), F32),
        ],
        scratch_shapes=[pltpu.VMEM((8, LANES), F32)],
        compiler_params=_cparams(("arbitrary",)),
        name="merge_router",
    )(o_a, o_b, proj, proj, x, gt1, w_oa, w_ob, w_out, g2, sc2, sh2, w_r, b_r)


def _pos_kernel(ri_ref, st_ref, pos_ref):
    ri = ri_ref[...].astype(F32)
    lane = lax.broadcasted_iota(jnp.int32, ri.shape, 1)
    lane_f = lane.astype(F32)
    starts = st_ref[...]
    p0 = jnp.sum(jnp.where(lane_f == ri[:, 0:1], starts, 0.0), axis=-1, keepdims=True) + ri[:, 2:3]
    p1 = jnp.sum(jnp.where(lane_f == ri[:, 1:2], starts, 0.0), axis=-1, keepdims=True) + ri[:, 3:4]
    pos_ref[...] = jnp.where(lane == 0, p0, jnp.where(lane == 1, p1, 0.0)).astype(jnp.int32)


def _positions(ri, starts_lanes):
    t = ri.shape[0]
    return pl.pallas_call(
        _pos_kernel,
        grid=(t // TM_POS,),
        in_specs=[
            pl.BlockSpec((TM_POS, LANES), lambda i: (i, 0)),
            pl.BlockSpec((1, LANES), lambda i: (0, 0)),
        ],
        out_specs=pl.BlockSpec((TM_POS, LANES), lambda i: (i, 0)),
        out_shape=jax.ShapeDtypeStruct((t, LANES), jnp.int32),
        compiler_params=_cparams(("arbitrary",)),
        name="moe_positions",
    )(ri, starts_lanes)


def _row_copy(src_ref, src_row, dst_ref, dst_row, sem):
    return pltpu.make_async_copy(src_ref.at[pl.ds(src_row, 1)], dst_ref.at[pl.ds(dst_row, 1)], sem)


def _dispatch_kernel(pos_ref, ends_ref, h_ref, xs_ref, zero_s, sem, zsem):
    n_tok = h_ref.shape[0]

    @pl.when(pl.program_id(0) == 0)
    def _():
        zero_s[...] = jnp.zeros_like(zero_s)

        def last_tile(e):
            end = ends_ref[e]
            begin = jnp.where(e > 0, ends_ref[jnp.maximum(e - 1, 0)], 0)
            dst = xs_ref.at[pl.ds(pl.multiple_of(jnp.maximum(end - TM_EXPERT, 0), TM_EXPERT), TM_EXPERT)]
            return end > begin, pltpu.make_async_copy(zero_s, dst, zsem)

        def fill(e, carry):
            nonempty, cp = last_tile(e)

            @pl.when(nonempty)
            def _():
                cp.start()

            return carry

        def fill_done(e, carry):
            nonempty, cp = last_tile(e)

            @pl.when(nonempty)
            def _():
                cp.wait()

            return carry

        n_used = ends_ref[MOE_EXPERTS - 1] // TM_EXPERT

        def unused_tile(tile):
            dst = xs_ref.at[pl.ds(pl.multiple_of(tile * TM_EXPERT, TM_EXPERT), TM_EXPERT)]
            return pltpu.make_async_copy(zero_s, dst, zsem)

        def tail(tile, carry):
            unused_tile(tile).start()
            return carry

        def tail_done(tile, carry):
            unused_tile(tile).wait()
            return carry

        n_tiles = xs_ref.shape[0] // TM_EXPERT
        lax.fori_loop(0, MOE_EXPERTS, fill, 0)
        lax.fori_loop(n_used, n_tiles, tail, 0)
        lax.fori_loop(0, MOE_EXPERTS, fill_done, 0)
        lax.fori_loop(n_used, n_tiles, tail_done, 0)

    def issue(tok, carry):
        _row_copy(h_ref, tok, xs_ref, pos_ref[2 * tok], sem).start()
        _row_copy(h_ref, tok, xs_ref, pos_ref[2 * tok + 1], sem).start()
        return carry

    lax.fori_loop(0, n_tok, issue, 0, unroll=8)

    def drain(tok, carry):
        _row_copy(h_ref, 0, xs_ref, 0, sem).wait()
        _row_copy(h_ref, 0, xs_ref, 0, sem).wait()
        return carry

    lax.fori_loop(0, n_tok, drain, 0, unroll=8)


def _dispatch(pos_flat, ends, h2, n_rows):
    t, d = h2.shape
    return pl.pallas_call(
        _dispatch_kernel,
        grid=(t // DISPATCH_BLK,),
        in_specs=[
            pl.BlockSpec((2 * DISPATCH_BLK,), lambda i: (i,), memory_space=pltpu.SMEM),
            pl.BlockSpec(memory_space=pltpu.SMEM),
            pl.BlockSpec((DISPATCH_BLK, d), lambda i: (i, 0)),
        ],
        out_specs=pl.BlockSpec(memory_space=pl.ANY),
        out_shape=jax.ShapeDtypeStruct((n_rows, d), h2.dtype),
        scratch_shapes=[pltpu.VMEM((TM_EXPERT, d), h2.dtype), pltpu.SemaphoreType.DMA(()),
                        pltpu.SemaphoreType.DMA(())],
        compiler_params=_cparams(("arbitrary",)),
        name="moe_dispatch",
    )(pos_flat, ends, h2)


def _expert_kernel(te_ref, nv_ref, x_ref, w1_ref, w3_ref, w2_ref, o_ref, w13_s, w2_s):
    i = pl.program_id(0)
    valid = i < nv_ref[0]
    new_expert = (i == 0) | (te_ref[i] != te_ref[jnp.maximum(i - 1, 0)])

    @pl.when(valid & new_expert)
    def _():
        w13_s[:, :MOE_DFF] = w1_ref[0].astype(BF16)
        w13_s[:, MOE_DFF:] = w3_ref[0].astype(BF16)
        w2_s[...] = w2_ref[0].astype(BF16)

    @pl.when(valid)
    def _():
        half = w13_s.shape[0] // 2
        x_lo, x_hi = _unpack_halves(x_ref[...])
        ab = _mm(x_lo.astype(BF16), w13_s[:half, :]) + _mm(x_hi.astype(BF16), w13_s[half:, :])
        a = ab[:, :MOE_DFF]
        o_ref[...] = _pack_halves(_mm((a * _sig(a) * ab[:, MOE_DFF:]).astype(BF16), w2_s[...]))

    @pl.when(jnp.logical_not(valid))
    def _():
        o_ref[...] = jnp.zeros_like(o_ref)


def _experts(tile_expert, n_valid, xs, w1, w3, w2, layer):
    n_rows, dp = xs.shape
    d = 2 * dp
    tm = TM_EXPERT
    first = layer * MOE_EXPERTS
    grid_spec = pltpu.PrefetchScalarGridSpec(
        num_scalar_prefetch=2,
        grid=(n_rows // tm,),
        in_specs=[
            pl.BlockSpec((tm, dp), lambda i, te, nv: (jnp.minimum(i, nv[0] - 1), 0)),
            pl.BlockSpec((1, d, MOE_DFF), lambda i, te, nv: (first + te[i], 0, 0)),
            pl.BlockSpec((1, d, MOE_DFF), lambda i, te, nv: (first + te[i], 0, 0)),
            pl.BlockSpec((1, MOE_DFF, d), lambda i, te, nv: (first + te[i], 0, 0)),
        ],
        out_specs=pl.BlockSpec((tm, dp), lambda i, te, nv: (i, 0)),
        scratch_shapes=[pltpu.VMEM((d, 2 * MOE_DFF), BF16), pltpu.VMEM((MOE_DFF, d), BF16)],
    )
    return pl.pallas_call(
        _expert_kernel,
        grid_spec=grid_spec,
        out_shape=jax.ShapeDtypeStruct((n_rows, dp), jnp.uint32),
        compiler_params=_cparams(("arbitrary",)),
        name="moe_experts",
    )(tile_expert, n_valid, xs, w1, w3, w2)


def _combine_kernel(pos_ref, pos_next_ref, ys_ref, x_ref, rw_ref, gt_ref, fg_ref, o_ref, buf, sem, *, final):
    tm = x_ref.shape[0]
    i = pl.program_id(0)
    slot = i % 2

    def gather(p_ref, s):
        def issue(tok, carry):
            _row_copy(ys_ref, p_ref[2 * tok], buf.at[s, 0], tok, sem.at[s]).start()
            _row_copy(ys_ref, p_ref[2 * tok + 1], buf.at[s, 1], tok, sem.at[s]).start()
            return carry

        lax.fori_loop(0, tm, issue, 0, unroll=8)

    @pl.when(i == 0)
    def _():
        gather(pos_ref, 0)

    @pl.when(i + 1 < pl.num_programs(0))
    def _():
        gather(pos_next_ref, 1 - slot)

    def drain(tok, carry):
        _row_copy(ys_ref, 0, buf.at[slot, 0], 0, sem.at[slot]).wait()
        _row_copy(ys_ref, 0, buf.at[slot, 1], 0, sem.at[slot]).wait()
        return carry

    lax.fori_loop(0, tm, drain, 0, unroll=8)

    rw = rw_ref[...]
    w0, w1 = rw[:, 0:1], rw[:, 1:2]
    y0_lo, y0_hi = _unpack_halves(buf[slot, 0])
    y1_lo, y1_hi = _unpack_halves(buf[slot, 1])
    y = jnp.concatenate([w0 * y0_lo + w1 * y1_lo, w0 * y0_hi + w1 * y1_hi], axis=1)
    out = x_ref[...] + gt_ref[0] * y
    if final:
        out = _rms(out) * fg_ref[...]
    o_ref[...] = out


def _combine(pos_flat, ys, x1, rw, gt2, final_g, seq, final):
    t, d = x1.shape
    tm = TM_COMBINE
    tpb = seq // tm
    last = t // tm - 1
    return pl.pallas_call(
        functools.partial(_combine_kernel, final=final),
        grid=(t // tm,),
        in_specs=[
            pl.BlockSpec((2 * tm,), lambda i: (i,), memory_space=pltpu.SMEM),
            pl.BlockSpec((2 * tm,), lambda i: (jnp.minimum(i + 1, last),), memory_space=pltpu.SMEM),
            pl.BlockSpec(memory_space=pl.ANY),
            pl.BlockSpec((tm, d), lambda i: (i, 0)),
            pl.BlockSpec((tm, LANES), lambda i: (i, 0)),
            pl.BlockSpec((1, 1, d), lambda i: (i // tpb, 0, 0)),
            pl.BlockSpec((1, d), lambda i: (0, 0)),
        ],
        out_specs=pl.BlockSpec((tm, d), lambda i: (i, 0)),
        out_shape=jax.ShapeDtypeStruct((t, d), F32),
        scratch_shapes=[pltpu.VMEM((2, 2, tm, d // 2), jnp.uint32), pltpu.SemaphoreType.DMA((2,))],
        compiler_params=_cparams(("arbitrary",)),
        name="moe_combine",
    )(pos_flat, pos_flat, ys, x1, rw, gt2, final_g)


def _routing_tables(ri, cnt, n_tiles):
    counts = cnt[0, RL_E:RL_E + MOE_EXPERTS].astype(jnp.int32)
    padded = ((counts + TM_EXPERT - 1) // TM_EXPERT) * TM_EXPERT
    ends = jnp.cumsum(padded)
    starts = ends - padded
    starts_lanes = jnp.concatenate(
        [starts.astype(F32), jnp.zeros((LANES - MOE_EXPERTS,), F32)]).reshape(1, LANES)
    pos = _positions(ri, starts_lanes)[:, :2].reshape(-1)
    n_valid = (ends[-1] // TM_EXPERT).astype(jnp.int32).reshape(1)
    tile_start = jnp.arange(n_tiles, dtype=jnp.int32) * TM_EXPERT
    tile_expert = jnp.minimum(
        jnp.sum((tile_start[:, None] >= ends[None, :]).astype(jnp.int32), axis=1), MOE_EXPERTS - 1)
    return pos, ends, tile_expert, n_valid


def kernel(x, c, ada_w, ada_b, norm1_g, w_in, gla_w_gk2, gla_b_gk, gla_norm_g, conv_w, conv_b,
           dt_bias, a_log, d_skip, ssm_norm_g, w_oa, w_ob, w_out, norm2_g, router_group_w,
           router_group_b, router_expert_w, router_expert_b, expert_w1, expert_w3, expert_w2,
           final_norm_g):
    bsz, seq, d = x.shape
    depth = ada_w.shape[0]
    t = bsz * seq
    n_tiles = (2 * t) // TM_EXPERT + MOE_EXPERTS
    n_rows = n_tiles * TM_EXPERT

    mod = _modulation(c, ada_w, ada_b)
    xf = x.reshape(t, d)
    c_gk, c_dt = OFF_G + GLA_DV, OFF_C + SSM_GROUPS * SSM_DSTATE + GLA_RANK
    for l in range(depth):
        sh1, sc1, gt1, sh2, sc2, gt2 = [mod[l, :, k * d:(k + 1) * d].reshape(bsz, 1, d) for k in range(N_MOD)]
        w = w_in[l]
        w_main = jnp.concatenate(
            [w[:, :c_gk], w[:, c_gk + GLA_RANK:c_dt], w[:, c_dt + SSM_HEADS:]], axis=1).astype(BF16)
        pad = jnp.zeros((d, LANES - GLA_RANK - SSM_HEADS), F32)
        w_small = jnp.concatenate([w[:, c_gk:c_gk + GLA_RANK], w[:, c_dt:c_dt + SSM_HEADS], pad], axis=1)
        dt_bias_pad = jnp.concatenate([jnp.zeros((GLA_RANK,), F32), dt_bias[l], pad[0]]).reshape(1, LANES)
        proj, small = _norm_proj(xf, norm1_g[l].reshape(1, d), sc1, sh1, w_main, w_small, dt_bias_pad, seq)

        wg_pad = jnp.concatenate([gla_w_gk2[l], jnp.zeros((LANES - GLA_RANK, GLA_DK), F32)], axis=0)
        o_a = _gla(proj, small, wg_pad, gla_b_gk[l].reshape(1, GLA_DK),
                   gla_norm_g[l].reshape(1, GLA_HV), bsz, seq)
        o_b = _ssd(proj, small, conv_w[l], conv_b[l].reshape(1, -1),
                   jnp.repeat(a_log[l], SSM_HEADDIM).reshape(1, SSM_DINNER),
                   jnp.repeat(d_skip[l], SSM_HEADDIM).reshape(1, SSM_DINNER),
                   ssm_norm_g[l].reshape(1, SSM_DINNER), bsz, seq)

        w_r = jnp.concatenate([router_group_w[l], router_expert_w[l],
                               jnp.zeros((d, LANES - MOE_GROUPS - MOE_EXPERTS), F32)], axis=1)
        b_r = jnp.concatenate([router_group_b[l], router_expert_b[l],
                               jnp.zeros((LANES - MOE_GROUPS - MOE_EXPERTS,), F32)]).reshape(1, LANES)
        x1, h2, ri, rw, cnt = _merge(o_a, o_b, proj, xf, gt1, w_oa[l].astype(BF16), w_ob[l].astype(BF16),
                                     w_out[l].astype(BF16), norm2_g[l].reshape(1, d), sc2, sh2, w_r, b_r, seq)

        pos, ends, tile_expert, n_valid = _routing_tables(ri, cnt, n_tiles)
        xs = _dispatch(pos, ends, h2, n_rows)
        ys = _experts(tile_expert, n_valid, xs,
                      expert_w1.reshape(depth * MOE_EXPERTS, d, MOE_DFF),
                      expert_w3.reshape(depth * MOE_EXPERTS, d, MOE_DFF),
                      expert_w2.reshape(depth * MOE_EXPERTS, MOE_DFF, d), l)
        xf = _combine(pos, ys, x1, rw, gt2, final_norm_g.reshape(1, d), seq, final=(l == depth - 1))
    return xf.reshape(bsz, seq, d)
```

```python
import functools

import jax
import jax.numpy as jnp
from jax import lax
from jax.experimental import pallas as pl
from jax.experimental.pallas import tpu as pltpu

F32 = jnp.float32
BF16 = jnp.bfloat16

EPS = 1e-6
CHUNK = 64
D_MODEL = 1024
N_MOD = 6
GLA_HEADS = 4
GLA_HK = 128
GLA_HV = 256
GLA_DK = GLA_HEADS * GLA_HK
GLA_DV = GLA_HEADS * GLA_HV
GLA_RANK = 16
GLA_NORMALIZER = 16.0
SSM_DINNER = 2048
SSM_HEADDIM = 64
SSM_HEADS = SSM_DINNER // SSM_HEADDIM
SSM_GROUPS = 8
SSM_HPG = SSM_HEADS // SSM_GROUPS
SSM_GW = SSM_HPG * SSM_HEADDIM
SSM_DSTATE = 128
SSM_CONV = 4
MOE_GROUPS = 4
MOE_EPG = 8
MOE_EXPERTS = MOE_GROUPS * MOE_EPG
MOE_DFF = 256

OFF_Q = 0
OFF_K = OFF_Q + GLA_DK
OFF_V = OFF_K + GLA_DK
OFF_G = OFF_V + GLA_DV
OFF_Z = OFF_G + GLA_DV
OFF_XS = OFF_Z + SSM_DINNER
OFF_B = OFF_XS + SSM_DINNER
OFF_C = OFF_B + SSM_GROUPS * SSM_DSTATE
OFF_GA = OFF_C + SSM_GROUPS * SSM_DSTATE
OFF_GB = OFF_GA + D_MODEL
N_PROJ = OFF_GB + D_MODEL
LANES = 128
SM_DT = GLA_RANK
RL_E = MOE_GROUPS

V7X_VMEM_LIMIT = 56 * 1024 * 1024

TM_PROJ = 1024
NJ_PROJ = 4
TN_PROJ = N_PROJ // NJ_PROJ
TM_MERGE = 512
TM_EXPERT = 512
TM_COMBINE = 256
TM_POS = 2048
DISPATCH_BLK = 1024
MIX_BLK = 256
CPB = MIX_BLK // CHUNK
CONV_ROWS = 128
CONV_HALO = 16
CONV_WIN = CONV_HALO + CONV_ROWS


def _cparams(sem):
    return pltpu.CompilerParams(dimension_semantics=sem, vmem_limit_bytes=V7X_VMEM_LIMIT)


def _sig(x):
    return 0.5 * jnp.tanh(0.5 * x) + 0.5


def _softplus(x):
    return jnp.maximum(x, 0.0) + jnp.log(1.0 + jnp.exp(-jnp.abs(x)))


def _log_sigmoid(x):
    return jnp.minimum(x, 0.0) - jnp.log(1.0 + jnp.exp(-jnp.abs(x)))


def _split(a):
    hi = a.astype(BF16)
    lo = (a - hi.astype(F32)).astype(BF16)
    return hi, lo


def _mm(a, b):
    return jnp.dot(a, b, preferred_element_type=F32)


def _dot3(a, b):
    ah, al = _split(a)
    bh, bl = _split(b)
    return _mm(ah, bh) + _mm(al, bh) + _mm(ah, bl)


def _dot2_lhs(a, b_bf16):
    ah, al = _split(a)
    return _mm(ah, b_bf16) + _mm(al, b_bf16)


def _chunk_scan_matrix(n):
    r = lax.broadcasted_iota(jnp.int32, (n, n), 0)
    c = lax.broadcasted_iota(jnp.int32, (n, n), 1)
    return jnp.where((r // CHUNK == c // CHUNK) & (r >= c), 1.0, 0.0).astype(BF16)


def _chunk_scan(mat, x):
    n, w = x.shape
    hi, lo = _split(x)
    out = _mm(mat, jnp.concatenate([hi, lo], axis=1))
    cum = out[:, :w] + out[:, w:]
    last_rows = [cum[c * CHUNK + CHUNK - 1:(c + 1) * CHUNK, :] for c in range(n // CHUNK)]
    tot = jnp.concatenate([jnp.broadcast_to(row, (CHUNK, w)) for row in last_rows], axis=0)
    return cum, tot


def _rms(x):
    return x * lax.rsqrt(jnp.mean(x * x, axis=-1, keepdims=True) + EPS)


def _pack_halves(x):
    w = x.shape[1] // 2
    lo = lax.bitcast_convert_type(x[:, :w].astype(BF16).astype(F32), jnp.uint32)
    hi = lax.bitcast_convert_type(x[:, w:].astype(BF16).astype(F32), jnp.uint32)
    return (lo >> 16) | (hi & jnp.uint32(0xFFFF0000))


def _unpack_halves(p):
    lo = lax.bitcast_convert_type(p << 16, F32)
    hi = lax.bitcast_convert_type(p & jnp.uint32(0xFFFF0000), F32)
    return lo, hi


def _skewed_loop(n, stages):
    depth = len(stages)

    def step(i, lo, hi):
        for k in reversed(range(lo, hi)):
            stages[k](i - k)

    for i in range(depth - 1):
        step(i, 0, i + 1)

    def body(i, carry):
        step(i, 0, depth)
        return carry

    lax.fori_loop(depth - 1, n, body, 0)
    for i in range(n, n + depth - 1):
        step(i, i - n + 1, depth)


def _item_rows(i):
    start = i * MIX_BLK
    return start if isinstance(start, int) else pl.multiple_of(start, MIX_BLK)


def _mod_kernel(c_ref, w_ref, b_ref, o_ref):
    c = c_ref[...]
    o_ref[0] = _dot3(c * _sig(c), w_ref[0]) + b_ref[0]


def _modulation(c, ada_w, ada_b):
    depth, d, n = ada_w.shape
    bsz = c.shape[0]
    tn = 1536
    return pl.pallas_call(
        _mod_kernel,
        grid=(depth, n // tn),
        in_specs=[
            pl.BlockSpec((bsz, d), lambda l, j: (0, 0)),
            pl.BlockSpec((1, d, tn), lambda l, j: (l, 0, j)),
            pl.BlockSpec((1, 1, tn), lambda l, j: (l, 0, j)),
        ],
        out_specs=pl.BlockSpec((1, bsz, tn), lambda l, j: (l, 0, j)),
        out_shape=jax.ShapeDtypeStruct((depth, bsz, n), F32),
        compiler_params=_cparams(("arbitrary", "arbitrary")),
        name="adaln_mod",
    )(c, ada_w, ada_b.reshape(depth, 1, n))


def _norm_proj_kernel(x0_ref, xn_ref, g_ref, sc0_ref, sh0_ref, scn_ref, shn_ref, w_ref, ws_ref, dtb_ref,
                      o_ref, os_ref, h_s, side_s):
    i = pl.program_id(0)
    j = pl.program_id(1)
    cur = i % 2
    rows_per_step = TM_PROJ // NJ_PROJ

    def norm_rows(x_ref, sc_ref, sh_ref, rows, slot):
        h = _rms(x_ref[rows, :]) * g_ref[...] * (1.0 + sc_ref[0]) + sh_ref[0]
        h_s[slot, rows, :] = h.astype(BF16)
        side = _dot3(h, ws_ref[...])
        lane = lax.broadcasted_iota(jnp.int32, side.shape, 1)
        is_dt = (lane >= SM_DT) & (lane < SM_DT + SSM_HEADS)
        side_s[slot, rows, :] = jnp.where(is_dt, _softplus(side + dtb_ref[...]), side)

    @pl.when((i == 0) & (j == 0))
    def _():
        for r in range(NJ_PROJ):
            norm_rows(x0_ref, sc0_ref, sh0_ref, pl.ds(r * rows_per_step, rows_per_step), 0)

    norm_rows(xn_ref, scn_ref, shn_ref,
              pl.ds(pl.multiple_of(j * rows_per_step, rows_per_step), rows_per_step), 1 - cur)

    @pl.when(j == 0)
    def _():
        os_ref[...] = side_s[cur]

    o_ref[...] = _mm(h_s[cur], w_ref[...]).astype(BF16)


def _norm_proj(x, g, sc, sh, w_main, w_small, dt_bias_pad, seq):
    t, d = x.shape
    tpb = seq // TM_PROJ
    last = t // TM_PROJ - 1
    nxt = lambda i: jnp.minimum(i + 1, last)
    return pl.pallas_call(
        _norm_proj_kernel,
        grid=(t // TM_PROJ, NJ_PROJ),
        in_specs=[
            pl.BlockSpec((TM_PROJ, d), lambda i, j: (0, 0)),
            pl.BlockSpec((TM_PROJ, d), lambda i, j: (nxt(i), 0)),
            pl.BlockSpec((1, d), lambda i, j: (0, 0)),
            pl.BlockSpec((1, 1, d), lambda i, j: (0, 0, 0)),
            pl.BlockSpec((1, 1, d), lambda i, j: (0, 0, 0)),
            pl.BlockSpec((1, 1, d), lambda i, j: (nxt(i) // tpb, 0, 0)),
            pl.BlockSpec((1, 1, d), lambda i, j: (nxt(i) // tpb, 0, 0)),
            pl.BlockSpec((d, TN_PROJ), lambda i, j: (0, j)),
            pl.BlockSpec((d, LANES), lambda i, j: (0, 0)),
            pl.BlockSpec((1, LANES), lambda i, j: (0, 0)),
        ],
        out_specs=[
            pl.BlockSpec((TM_PROJ, TN_PROJ), lambda i, j: (i, j)),
            pl.BlockSpec((TM_PROJ, LANES), lambda i, j: (i, 0)),
        ],
        out_shape=[
            jax.ShapeDtypeStruct((t, N_PROJ), BF16),
            jax.ShapeDtypeStruct((t, LANES), F32),
        ],
        scratch_shapes=[pltpu.VMEM((2, TM_PROJ, d), BF16), pltpu.VMEM((2, TM_PROJ, LANES), F32)],
        compiler_params=_cparams(("arbitrary", "arbitrary")),
        name="norm_in_proj",
    )(x, x, g, sc, sh, sc, sh, w_main, w_small, dt_bias_pad)


def _gla_kernel(q_ref, k_ref, v_ref, g_ref, sm_ref, wg_ref, bg_ref, ng_ref, o_ref,
                gk_s, kdec_s, edec_s, upd_s, st_s, oraw_s):
    seq = q_ref.shape[0]
    scan = _chunk_scan_matrix(MIX_BLK)
    wg_hi, wg_lo = _split(wg_ref[...])
    bg = bg_ref[...]
    ng = ng_ref[...]
    scale = GLA_HK ** -0.5
    st_s[...] = jnp.zeros_like(st_s)

    def stage_gate(i):
        rows = pl.ds(_item_rows(i), MIX_BLK)
        sm_hi, sm_lo = _split(sm_ref[rows, :])
        pre = _mm(sm_hi, wg_hi) + _mm(sm_lo, wg_hi) + _mm(sm_hi, wg_lo) + bg
        gk_s[rows, :] = _log_sigmoid(pre) * (1.0 / GLA_NORMALIZER)

    def stage_decay(i):
        rows = pl.ds(_item_rows(i), MIX_BLK)
        cum, tot = _chunk_scan(scan, gk_s[rows, :])
        kdec_s[rows, :] = (k_ref[rows, :].astype(F32) * jnp.exp(tot - cum)).astype(BF16)
        edec_s[rows, :] = jnp.exp(tot)

    def stage_update(i):
        for c in range(CPB):
            rows = pl.ds(_item_rows(i) + c * CHUNK, CHUNK)
            upd_s[i * CPB + c] = lax.dot_general(
                v_ref[rows, :], kdec_s[rows, :], (((0,), (0,)), ((), ())), preferred_element_type=F32)

    def stage_state(i):
        st = st_s[...]
        for c in range(CPB):
            r0 = _item_rows(i) + c * CHUNK
            st = edec_s[pl.ds(r0, 1), :] * st + upd_s[i * CPB + c]
            oraw_s[pl.ds(r0, CHUNK), :] = lax.dot_general(
                q_ref[pl.ds(r0, CHUNK), :], st.astype(BF16), (((1,), (1,)), ((), ())),
                preferred_element_type=F32)
        st_s[...] = st

    def stage_out(i):
        rows = pl.ds(_item_rows(i), MIX_BLK)
        o = _rms(oraw_s[rows, :] * scale) * ng
        g = g_ref[rows, :].astype(F32)
        o_ref[rows, :] = (o * (g * _sig(g))).astype(BF16)

    _skewed_loop(seq // MIX_BLK, [stage_gate, stage_decay, stage_update, stage_state, stage_out])


def _gla(proj, small, wg_pad, b_gk, norm_g, bsz, seq):
    t = proj.shape[0]
    kq, kk = OFF_Q // GLA_HK, OFF_K // GLA_HK
    kv, kg = OFF_V // GLA_HV, OFF_G // GLA_HV
    return pl.pallas_call(
        _gla_kernel,
        grid=(bsz, GLA_HEADS),
        in_specs=[
            pl.BlockSpec((seq, GLA_HK), lambda b, h: (b, kq + h)),
            pl.BlockSpec((seq, GLA_HK), lambda b, h: (b, kk + h)),
            pl.BlockSpec((seq, GLA_HV), lambda b, h: (b, kv + h)),
            pl.BlockSpec((seq, GLA_HV), lambda b, h: (b, kg + h)),
            pl.BlockSpec((seq, LANES), lambda b, h: (b, 0)),
            pl.BlockSpec((LANES, GLA_HK), lambda b, h: (0, h)),
            pl.BlockSpec((1, GLA_HK), lambda b, h: (0, h)),
            pl.BlockSpec((1, GLA_HV), lambda b, h: (0, 0)),
        ],
        out_specs=pl.BlockSpec((seq, GLA_HV), lambda b, h: (b, h)),
        out_shape=jax.ShapeDtypeStruct((t, GLA_DV), BF16),
        scratch_shapes=[
            pltpu.VMEM((seq, GLA_HK), F32),
            pltpu.VMEM((seq, GLA_HK), BF16),
            pltpu.VMEM((seq, GLA_HK), F32),
            pltpu.VMEM((seq // CHUNK, GLA_HV, GLA_HK), F32),
            pltpu.VMEM((GLA_HV, GLA_HK), F32),
            pltpu.VMEM((seq, GLA_HV), F32),
        ],
        compiler_params=_cparams(("arbitrary", "arbitrary")),
        name="gla_mixer",
    )(proj, proj, proj, proj, small, wg_pad, b_gk, norm_g)


def _conv_shift_matrix():
    t = lax.broadcasted_iota(jnp.int32, (CONV_ROWS, SSM_CONV * CONV_WIN), 0)
    c = lax.broadcasted_iota(jnp.int32, (CONV_ROWS, SSM_CONV * CONV_WIN), 1)
    return jnp.where(c % CONV_WIN == t + CONV_HALO - c // CONV_WIN, 1.0, 0.0).astype(BF16)


def _conv_window(ref, r0):
    if isinstance(r0, int) and r0 == 0:
        return jnp.concatenate([jnp.zeros((CONV_HALO, ref.shape[1]), BF16), ref[0:CONV_ROWS, :]], axis=0)
    start = r0 - CONV_HALO
    return ref[pl.ds(start if isinstance(start, int) else pl.multiple_of(start, CONV_HALO), CONV_WIN), :]


def _ssd_kernel(z_ref, xs_ref, b_ref, c_ref, sm_ref, cwx_ref, cwb_ref, cwc_ref, cbx_ref, cbb_ref,
                cbc_ref, alog_ref, dsk_ref, ng_ref, o_ref,
                xs_s, b_s, c_s, dt_s, a_s, ea_s, edec_s, xdt_s, decx_s, m_s, upd_s, st_s, y_s):
    seq = z_ref.shape[0]
    grp = pl.program_id(1)
    er = lax.broadcasted_iota(jnp.int32, (LANES, SSM_GW), 0)
    el = lax.broadcasted_iota(jnp.int32, (LANES, SSM_GW), 1)
    expand = jnp.where(er == SM_DT + SSM_HPG * grp + el // SSM_HEADDIM, 1.0, 0.0).astype(BF16)
    scan = _chunk_scan_matrix(MIX_BLK)
    a_row = -jnp.exp(alog_ref[...])
    dsk = dsk_ref[...]
    ng = ng_ref[...]
    dr = lax.broadcasted_iota(jnp.int32, (CHUNK, SSM_GW), 0)
    dl = lax.broadcasted_iota(jnp.int32, (CHUNK, SSM_GW), 1)
    diag = jnp.where(dr == dl % CHUNK, 1.0, 0.0)
    br = lax.broadcasted_iota(jnp.int32, (SSM_GW, SSM_GW), 0)
    bl = lax.broadcasted_iota(jnp.int32, (SSM_GW, SSM_GW), 1)
    blockmask = jnp.where(br // SSM_HEADDIM == bl // SSM_HEADDIM, 1.0, 0.0).astype(BF16)
    st_s[...] = jnp.zeros_like(st_s)
    shift = _conv_shift_matrix()
    taps = jnp.concatenate([cwx_ref[...], cwb_ref[...], cwc_ref[...]], axis=1)
    conv_bias = jnp.concatenate([cbx_ref[...], cbb_ref[...], cbc_ref[...]], axis=1)

    def stage_conv(i):
        rows = pl.ds(_item_rows(i), MIX_BLK)
        dt_s[rows, :] = _dot2_lhs(sm_ref[rows, :], expand)
        for sub in range(MIX_BLK // CONV_ROWS):
            r0 = _item_rows(i) + sub * CONV_ROWS
            win = jnp.concatenate(
                [_conv_window(xs_ref, r0), _conv_window(b_ref, r0), _conv_window(c_ref, r0)], axis=1)
            scaled = jnp.concatenate(
                [win * taps[SSM_CONV - 1 - k:SSM_CONV - k, :].astype(BF16) for k in range(SSM_CONV)], axis=0)
            acc = _mm(shift, scaled) + conv_bias
            act = acc * _sig(acc)
            out_rows = pl.ds(r0, CONV_ROWS)
            xs_s[out_rows, :] = act[:, :SSM_GW]
            b_s[out_rows, :] = act[:, SSM_GW:SSM_GW + SSM_DSTATE].astype(BF16)
            c_s[out_rows, :] = act[:, SSM_GW + SSM_DSTATE:].astype(BF16)

    def stage_decay(i):
        rows = pl.ds(_item_rows(i), MIX_BLK)
        dt = dt_s[rows, :]
        acum, atot = _chunk_scan(scan, dt * a_row)
        a_s[rows, :] = acum
        ea_s[rows, :] = jnp.exp(acum)
        edec_s[rows, :] = jnp.exp(atot)
        xdt = xs_s[rows, :] * dt
        xdt_s[rows, :] = xdt.astype(BF16)
        decx_s[rows, :] = (xdt * jnp.exp(atot - acum)).astype(BF16)

    def stage_intra(i):
        for c in range(CPB):
            rows = pl.ds(_item_rows(i) + c * CHUNK, CHUNK)
            acum = a_s[rows, :]
            acum_s = jnp.sum(acum * diag, axis=0, keepdims=True)
            lmask = jnp.exp(-jnp.abs(acum - acum_s))
            bm = b_s[rows, :]
            cb = lax.dot_general(c_s[rows, :], jnp.concatenate([bm] * SSM_HPG, axis=0),
                                 (((1,), (1,)), ((), ())), preferred_element_type=F32)
            m_s[rows, :] = (cb * lmask).astype(BF16)
            upd_s[i * CPB + c] = lax.dot_general(bm, decx_s[rows, :], (((0,), (0,)), ((), ())),
                                                 preferred_element_type=F32)

    def stage_state(i):
        st = st_s[...]
        for c in range(CPB):
            r0 = _item_rows(i) + c * CHUNK
            rows = pl.ds(r0, CHUNK)
            xdt_bd = jnp.concatenate([xdt_s[rows, :]] * SSM_HPG, axis=0) * blockmask
            y = _mm(m_s[rows, :], xdt_bd) + _mm(c_s[rows, :], st.astype(BF16)) * ea_s[rows, :]
            st = edec_s[pl.ds(r0, 1), :] * st + upd_s[i * CPB + c]
            y = y + xs_s[rows, :] * dsk
            z = z_ref[rows, :].astype(F32)
            y_s[rows, :] = y * (z * _sig(z))
        st_s[...] = st

    def stage_out(i):
        rows = pl.ds(_item_rows(i), MIX_BLK)
        o_ref[rows, :] = (_rms(y_s[rows, :]) * ng).astype(BF16)

    _skewed_loop(seq // MIX_BLK, [stage_conv, stage_decay, stage_intra, stage_state, stage_out])


def _ssd(proj, small, conv_w, conv_b, a_log_e, d_skip_e, norm_g, bsz, seq):
    t = proj.shape[0]
    kz, kx = OFF_Z // SSM_GW, OFF_XS // SSM_GW
    kb, kc = OFF_B // SSM_DSTATE, OFF_C // SSM_DSTATE
    nb = SSM_DINNER // SSM_DSTATE
    ncg = SSM_GROUPS
    wide_f32 = pltpu.VMEM((seq, SSM_GW), F32)
    wide_bf16 = pltpu.VMEM((seq, SSM_GW), BF16)
    return pl.pallas_call(
        _ssd_kernel,
        grid=(bsz, SSM_GROUPS),
        in_specs=[
            pl.BlockSpec((seq, SSM_GW), lambda b, g: (b, kz + g)),
            pl.BlockSpec((seq, SSM_GW), lambda b, g: (b, kx + g)),
            pl.BlockSpec((seq, SSM_DSTATE), lambda b, g: (b, kb + g)),
            pl.BlockSpec((seq, SSM_DSTATE), lambda b, g: (b, kc + g)),
            pl.BlockSpec((seq, LANES), lambda b, g: (b, 0)),
            pl.BlockSpec((SSM_CONV, SSM_GW), lambda b, g: (0, g)),
            pl.BlockSpec((SSM_CONV, SSM_DSTATE), lambda b, g: (0, nb + g)),
            pl.BlockSpec((SSM_CONV, SSM_DSTATE), lambda b, g: (0, nb + ncg + g)),
            pl.BlockSpec((1, SSM_GW), lambda b, g: (0, g)),
            pl.BlockSpec((1, SSM_DSTATE), lambda b, g: (0, nb + g)),
            pl.BlockSpec((1, SSM_DSTATE), lambda b, g: (0, nb + ncg + g)),
            pl.BlockSpec((1, SSM_GW), lambda b, g: (0, g)),
            pl.BlockSpec((1, SSM_GW), lambda b, g: (0, g)),
            pl.BlockSpec((1, SSM_GW), lambda b, g: (0, g)),
        ],
        out_specs=pl.BlockSpec((seq, SSM_GW), lambda b, g: (b, g)),
        out_shape=jax.ShapeDtypeStruct((t, SSM_DINNER), BF16),
        scratch_shapes=[
            wide_f32,
            pltpu.VMEM((seq, SSM_DSTATE), BF16),
            pltpu.VMEM((seq, SSM_DSTATE), BF16),
            wide_f32,
            wide_f32,
            wide_f32,
            wide_f32,
            wide_bf16,
            wide_bf16,
            wide_bf16,
            pltpu.VMEM((seq // CHUNK, SSM_DSTATE, SSM_GW), F32),
            pltpu.VMEM((SSM_DSTATE, SSM_GW), F32),
            wide_f32,
        ],
        compiler_params=_cparams(("arbitrary", "arbitrary")),
        name="ssd_mixer",
    )(proj, proj, proj, proj, small, conv_w, conv_w, conv_w, conv_b, conv_b, conv_b,
      a_log_e, d_skip_e, norm_g)


def _lane_min(x):
    return jnp.min(x, axis=-1, keepdims=True)


def _lane_max(x):
    return jnp.max(x, axis=-1, keepdims=True)


def _merge_kernel(oa_ref, ob_ref, ga_ref, gb_ref, x_ref, gt_ref, woa_ref, wob_ref, wout_ref,
                  g2_ref, sc_ref, sh_ref, wr_ref, br_ref,
                  x1_ref, h2_ref, ri_ref, rw_ref, cnt_ref, cnt_s, h_s):
    i = pl.program_id(0)
    tm = x_ref.shape[0]

    @pl.when(i == 0)
    def _():
        cnt_s[...] = jnp.zeros_like(cnt_s)
        h_s[...] = jnp.zeros_like(h_s)

    logit = _dot3(h_s[...], wr_ref[...]) + br_ref[...]

    ya = _mm(oa_ref[...], woa_ref[...])
    yb = _mm(ob_ref[...], wob_ref[...])

    lane = lax.broadcasted_iota(jnp.int32, (tm, LANES), 1)
    lane_f = lane.astype(F32)
    neg = jnp.float32(-jnp.inf)
    big = jnp.float32(LANES)
    is_g = lane < MOE_GROUPS
    lg = jnp.where(is_g, logit, neg)
    gmax = _lane_max(lg)
    gsum = jnp.sum(jnp.where(is_g, jnp.exp(lg - gmax), 0.0), axis=-1, keepdims=True)
    gidx = _lane_min(jnp.where(lg == gmax, lane_f, big))
    lo = RL_E + MOE_EPG * gidx
    in_grp = (lane_f >= lo) & (lane_f < lo + MOE_EPG)
    le = jnp.where(in_grp, logit, neg)
    emax = _lane_max(le)
    l0 = _lane_min(jnp.where(le == emax, lane_f, big))
    le2 = jnp.where(lane_f == l0, neg, le)
    emax2 = _lane_max(le2)
    l1 = _lane_min(jnp.where(le2 == emax2, lane_f, big))
    ratio = jnp.exp(emax2 - emax)
    w0 = 1.0 / (gsum * (1.0 + ratio))
    w1 = w0 * ratio

    counted = jnp.where(i > 0, 1.0, 0.0)
    oh0 = lane_f == l0
    oh1 = lane_f == l1
    oh = jnp.where(oh0 | oh1, counted, 0.0)
    tr = lax.broadcasted_iota(jnp.int32, (tm, tm), 0)
    tc = lax.broadcasted_iota(jnp.int32, (tm, tm), 1)
    before = _mm(jnp.where(tr > tc, 1.0, 0.0).astype(BF16), oh.astype(BF16)) + cnt_s[0:1, :]
    r0 = jnp.sum(jnp.where(oh0, before, 0.0), axis=-1, keepdims=True)
    r1 = jnp.sum(jnp.where(oh1, before, 0.0), axis=-1, keepdims=True)
    total = cnt_s[0:1, :] + jnp.sum(oh, axis=0, keepdims=True)
    cnt_s[...] = jnp.broadcast_to(total, cnt_s.shape)
    cnt_ref[...] = jnp.broadcast_to(total, cnt_ref.shape)

    ri = jnp.where(lane == 0, l0 - RL_E,
                   jnp.where(lane == 1, l1 - RL_E, jnp.where(lane == 2, r0, jnp.where(lane == 3, r1, 0.0))))
    ri_ref[...] = ri.astype(jnp.int32)
    rw_ref[...] = jnp.where(lane == 0, w0, jnp.where(lane == 1, w1, 0.0))

    merged = _sig(ga_ref[...].astype(F32)) * ya + _sig(gb_ref[...].astype(F32)) * yb
    x1 = x_ref[...] + gt_ref[0] * _mm(merged.astype(BF16), wout_ref[...])
    x1_ref[...] = x1
    h = _rms(x1) * g2_ref[...] * (1.0 + sc_ref[0]) + sh_ref[0]
    h2_ref[...] = _pack_halves(h)
    h_s[...] = h


def _merge(o_a, o_b, proj, x, gt1, w_oa, w_ob, w_out, g2, sc2, sh2, w_r, b_r, seq):
    t, d = x.shape
    tm = TM_MERGE
    tpb = seq // tm
    n = t // tm
    kga, kgb = OFF_GA // d, OFF_GB // d
    const = lambda i: (0, 0)
    cur = lambda i: jnp.minimum(i, n - 1)
    prev = lambda i: jnp.maximum(i - 1, 0)
    return pl.pallas_call(
        _merge_kernel,
        grid=(n + 1,),
        in_specs=[
            pl.BlockSpec((tm, GLA_DV), lambda i: (cur(i), 0)),
            pl.BlockSpec((tm, SSM_DINNER), lambda i: (cur(i), 0)),
            pl.BlockSpec((tm, d), lambda i: (cur(i), kga)),
            pl.BlockSpec((tm, d), lambda i: (cur(i), kgb)),
            pl.BlockSpec((tm, d), lambda i: (cur(i), 0)),
            pl.BlockSpec((1, 1, d), lambda i: (cur(i) // tpb, 0, 0)),
            pl.BlockSpec((GLA_DV, d), const),
            pl.BlockSpec((SSM_DINNER, d), const),
            pl.BlockSpec((d, d), const),
            pl.BlockSpec((1, d), const),
            pl.BlockSpec((1, 1, d), lambda i: (cur(i) // tpb, 0, 0)),
            pl.BlockSpec((1, 1, d), lambda i: (cur(i) // tpb, 0, 0)),
            pl.BlockSpec((d, LANES), const),
            pl.BlockSpec((1, LANES), const),
        ],
        out_specs=[
            pl.BlockSpec((tm, d), lambda i: (cur(i), 0)),
            pl.BlockSpec((tm, d // 2), lambda i: (cur(i), 0)),
            pl.BlockSpec((tm, LANES), lambda i: (prev(i), 0)),
            pl.BlockSpec((tm, LANES), lambda i: (prev(i), 0)),
            pl.BlockSpec((8, LANES), const),
        ],
        out_shape=[
            jax.ShapeDtypeStruct((t, d), F32),
            jax.ShapeDtypeStruct((t, d // 2), jnp.uint32),
            jax.ShapeDtypeStruct((t, LANES), jnp.int32),
            jax.ShapeDtypeStruct((t, LANES), F32),
            jax.ShapeDtypeStruct((8, LANES), F32),
        ],
        scratch_shapes=[pltpu.VMEM((8, LANES), F32), pltpu.VMEM((tm, d), F32)],
        compiler_params=_cparams(("arbitrary",)),
        name="merge_router",
    )(o_a, o_b, proj, proj, x, gt1, w_oa, w_ob, w_out, g2, sc2, sh2, w_r, b_r)


def _pos_kernel(ri_ref, st_ref, pos_ref):
    ri = ri_ref[...].astype(F32)
    lane = lax.broadcasted_iota(jnp.int32, ri.shape, 1)
    lane_f = lane.astype(F32)
    starts = st_ref[...]
    p0 = jnp.sum(jnp.where(lane_f == ri[:, 0:1], starts, 0.0), axis=-1, keepdims=True) + ri[:, 2:3]
    p1 = jnp.sum(jnp.where(lane_f == ri[:, 1:2], starts, 0.0), axis=-1, keepdims=True) + ri[:, 3:4]
    pos_ref[...] = jnp.where(lane == 0, p0, jnp.where(lane == 1, p1, 0.0)).astype(jnp.int32)


def _positions(ri, starts_lanes):
    t = ri.shape[0]
    return pl.pallas_call(
        _pos_kernel,
        grid=(t // TM_POS,),
        in_specs=[
            pl.BlockSpec((TM_POS, LANES), lambda i: (i, 0)),
            pl.BlockSpec((1, LANES), lambda i: (0, 0)),
        ],
        out_specs=pl.BlockSpec((TM_POS, LANES), lambda i: (i, 0)),
        out_shape=jax.ShapeDtypeStruct((t, LANES), jnp.int32),
        compiler_params=_cparams(("arbitrary",)),
        name="moe_positions",
    )(ri, starts_lanes)


def _row_copy(src_ref, src_row, dst_ref, dst_row, sem):
    return pltpu.make_async_copy(src_ref.at[pl.ds(src_row, 1)], dst_ref.at[pl.ds(dst_row, 1)], sem)


def _dispatch_kernel(pos_ref, ends_ref, h_ref, xs_ref, zero_s, sem, zsem):
    n_tok = h_ref.shape[0]

    @pl.when(pl.program_id(0) == 0)
    def _():
        zero_s[...] = jnp.zeros_like(zero_s)

        def last_tile(e):
            end = ends_ref[e]
            begin = jnp.where(e > 0, ends_ref[jnp.maximum(e - 1, 0)], 0)
            dst = xs_ref.at[pl.ds(pl.multiple_of(jnp.maximum(end - TM_EXPERT, 0), TM_EXPERT), TM_EXPERT)]
            return end > begin, pltpu.make_async_copy(zero_s, dst, zsem)

        def fill(e, carry):
            nonempty, cp = last_tile(e)

            @pl.when(nonempty)
            def _():
                cp.start()

            return carry

        def fill_done(e, carry):
            nonempty, cp = last_tile(e)

            @pl.when(nonempty)
            def _():
                cp.wait()

            return carry

        n_used = ends_ref[MOE_EXPERTS - 1] // TM_EXPERT

        def unused_tile(tile):
            dst = xs_ref.at[pl.ds(pl.multiple_of(tile * TM_EXPERT, TM_EXPERT), TM_EXPERT)]
            return pltpu.make_async_copy(zero_s, dst, zsem)

        def tail(tile, carry):
            unused_tile(tile).start()
            return carry

        def tail_done(tile, carry):
            unused_tile(tile).wait()
            return carry

        n_tiles = xs_ref.shape[0] // TM_EXPERT
        lax.fori_loop(0, MOE_EXPERTS, fill, 0)
        lax.fori_loop(n_used, n_tiles, tail, 0)
        lax.fori_loop(0, MOE_EXPERTS, fill_done, 0)
        lax.fori_loop(n_used, n_tiles, tail_done, 0)

    def issue(tok, carry):
        _row_copy(h_ref, tok, xs_ref, pos_ref[2 * tok], sem).start()
        _row_copy(h_ref, tok, xs_ref, pos_ref[2 * tok + 1], sem).start()
        return carry

    lax.fori_loop(0, n_tok, issue, 0, unroll=8)

    def drain(tok, carry):
        _row_copy(h_ref, 0, xs_ref, 0, sem).wait()
        _row_copy(h_ref, 0, xs_ref, 0, sem).wait()
        return carry

    lax.fori_loop(0, n_tok, drain, 0, unroll=8)


def _dispatch(pos_flat, ends, h2, n_rows):
    t, d = h2.shape
    return pl.pallas_call(
        _dispatch_kernel,
        grid=(t // DISPATCH_BLK,),
        in_specs=[
            pl.BlockSpec((2 * DISPATCH_BLK,), lambda i: (i,), memory_space=pltpu.SMEM),
            pl.BlockSpec(memory_space=pltpu.SMEM),
            pl.BlockSpec((DISPATCH_BLK, d), lambda i: (i, 0)),
        ],
        out_specs=pl.BlockSpec(memory_space=pl.ANY),
        out_shape=jax.ShapeDtypeStruct((n_rows, d), h2.dtype),
        scratch_shapes=[pltpu.VMEM((TM_EXPERT, d), h2.dtype), pltpu.SemaphoreType.DMA(()),
                        pltpu.SemaphoreType.DMA(())],
        compiler_params=_cparams(("arbitrary",)),
        name="moe_dispatch",
    )(pos_flat, ends, h2)


def _expert_kernel(te_ref, nv_ref, x_ref, w1_ref, w3_ref, w2_ref, o_ref, w13_s, w2_s):
    i = pl.program_id(0)
    valid = i < nv_ref[0]
    new_expert = (i == 0) | (te_ref[i] != te_ref[jnp.maximum(i - 1, 0)])

    @pl.when(valid & new_expert)
    def _():
        w13_s[:, :MOE_DFF] = w1_ref[0].astype(BF16)
        w13_s[:, MOE_DFF:] = w3_ref[0].astype(BF16)
        w2_s[...] = w2_ref[0].astype(BF16)

    @pl.when(valid)
    def _():
        half = w13_s.shape[0] // 2
        x_lo, x_hi = _unpack_halves(x_ref[...])
        ab = _mm(x_lo.astype(BF16), w13_s[:half, :]) + _mm(x_hi.astype(BF16), w13_s[half:, :])
        a = ab[:, :MOE_DFF]
        o_ref[...] = _pack_halves(_mm((a * _sig(a) * ab[:, MOE_DFF:]).astype(BF16), w2_s[...]))

    @pl.when(jnp.logical_not(valid))
    def _():
        o_ref[...] = jnp.zeros_like(o_ref)


def _experts(tile_expert, n_valid, xs, w1, w3, w2, layer):
    n_rows, dp = xs.shape
    d = 2 * dp
    tm = TM_EXPERT
    first = layer * MOE_EXPERTS
    grid_spec = pltpu.PrefetchScalarGridSpec(
        num_scalar_prefetch=2,
        grid=(n_rows // tm,),
        in_specs=[
            pl.BlockSpec((tm, dp), lambda i, te, nv: (jnp.minimum(i, nv[0] - 1), 0)),
            pl.BlockSpec((1, d, MOE_DFF), lambda i, te, nv: (first + te[i], 0, 0)),
            pl.BlockSpec((1, d, MOE_DFF), lambda i, te, nv: (first + te[i], 0, 0)),
            pl.BlockSpec((1, MOE_DFF, d), lambda i, te, nv: (first + te[i], 0, 0)),
        ],
        out_specs=pl.BlockSpec((tm, dp), lambda i, te, nv: (i, 0)),
        scratch_shapes=[pltpu.VMEM((d, 2 * MOE_DFF), BF16), pltpu.VMEM((MOE_DFF, d), BF16)],
    )
    return pl.pallas_call(
        _expert_kernel,
        grid_spec=grid_spec,
        out_shape=jax.ShapeDtypeStruct((n_rows, dp), jnp.uint32),
        compiler_params=_cparams(("arbitrary",)),
        name="moe_experts",
    )(tile_expert, n_valid, xs, w1, w3, w2)


def _combine_kernel(pos_ref, pos_next_ref, ys_ref, x_ref, rw_ref, gt_ref, fg_ref, o_ref, buf, sem, *, final):
    tm = x_ref.shape[0]
    i = pl.program_id(0)
    slot = i % 2

    def gather(p_ref, s):
        def issue(tok, carry):
            _row_copy(ys_ref, p_ref[2 * tok], buf.at[s, 0], tok, sem.at[s]).start()
            _row_copy(ys_ref, p_ref[2 * tok + 1], buf.at[s, 1], tok, sem.at[s]).start()
            return carry

        lax.fori_loop(0, tm, issue, 0, unroll=8)

    @pl.when(i == 0)
    def _():
        gather(pos_ref, 0)

    @pl.when(i + 1 < pl.num_programs(0))
    def _():
        gather(pos_next_ref, 1 - slot)

    def drain(tok, carry):
        _row_copy(ys_ref, 0, buf.at[slot, 0], 0, sem.at[slot]).wait()
        _row_copy(ys_ref, 0, buf.at[slot, 1], 0, sem.at[slot]).wait()
        return carry

    lax.fori_loop(0, tm, drain, 0, unroll=8)

    rw = rw_ref[...]
    w0, w1 = rw[:, 0:1], rw[:, 1:2]
    y0_lo, y0_hi = _unpack_halves(buf[slot, 0])
    y1_lo, y1_hi = _unpack_halves(buf[slot, 1])
    y = jnp.concatenate([w0 * y0_lo + w1 * y1_lo, w0 * y0_hi + w1 * y1_hi], axis=1)
    out = x_ref[...] + gt_ref[0] * y
    if final:
        out = _rms(out) * fg_ref[...]
    o_ref[...] = out


def _combine(pos_flat, ys, x1, rw, gt2, final_g, seq, final):
    t, d = x1.shape
    tm = TM_COMBINE
    tpb = seq // tm
    last = t // tm - 1
    return pl.pallas_call(
        functools.partial(_combine_kernel, final=final),
        grid=(t // tm,),
        in_specs=[
            pl.BlockSpec((2 * tm,), lambda i: (i,), memory_space=pltpu.SMEM),
            pl.BlockSpec((2 * tm,), lambda i: (jnp.minimum(i + 1, last),), memory_space=pltpu.SMEM),
            pl.BlockSpec(memory_space=pl.ANY),
            pl.BlockSpec((tm, d), lambda i: (i, 0)),
            pl.BlockSpec((tm, LANES), lambda i: (i, 0)),
            pl.BlockSpec((1, 1, d), lambda i: (i // tpb, 0, 0)),
            pl.BlockSpec((1, d), lambda i: (0, 0)),
        ],
        out_specs=pl.BlockSpec((tm, d), lambda i: (i, 0)),
        out_shape=jax.ShapeDtypeStruct((t, d), F32),
        scratch_shapes=[pltpu.VMEM((2, 2, tm, d // 2), jnp.uint32), pltpu.SemaphoreType.DMA((2,))],
        compiler_params=_cparams(("arbitrary",)),
        name="moe_combine",
    )(pos_flat, pos_flat, ys, x1, rw, gt2, final_g)


def _routing_tables(ri, cnt, n_tiles):
    counts = cnt[0, RL_E:RL_E + MOE_EXPERTS].astype(jnp.int32)
    padded = ((counts + TM_EXPERT - 1) // TM_EXPERT) * TM_EXPERT
    ends = jnp.cumsum(padded)
    starts = ends - padded
    starts_lanes = jnp.concatenate(
        [starts.astype(F32), jnp.zeros((LANES - MOE_EXPERTS,), F32)]).reshape(1, LANES)
    pos = _positions(ri, starts_lanes)[:, :2].reshape(-1)
    n_valid = (ends[-1] // TM_EXPERT).astype(jnp.int32).reshape(1)
    tile_start = jnp.arange(n_tiles, dtype=jnp.int32) * TM_EXPERT
    tile_expert = jnp.minimum(
        jnp.sum((tile_start[:, None] >= ends[None, :]).astype(jnp.int32), axis=1), MOE_EXPERTS - 1)
    return pos, ends, tile_expert, n_valid


def kernel(x, c, ada_w, ada_b, norm1_g, w_in, gla_w_gk2, gla_b_gk, gla_norm_g, conv_w, conv_b,
           dt_bias, a_log, d_skip, ssm_norm_g, w_oa, w_ob, w_out, norm2_g, router_group_w,
           router_group_b, router_expert_w, router_expert_b, expert_w1, expert_w3, expert_w2,
           final_norm_g):
    bsz, seq, d = x.shape
    depth = ada_w.shape[0]
    t = bsz * seq
    n_tiles = (2 * t) // TM_EXPERT + MOE_EXPERTS
    n_rows = n_tiles * TM_EXPERT

    mod = _modulation(c, ada_w, ada_b)
    xf = x.reshape(t, d)
    c_gk, c_dt = OFF_G + GLA_DV, OFF_C + SSM_GROUPS * SSM_DSTATE + GLA_RANK
    for l in range(depth):
        sh1, sc1, gt1, sh2, sc2, gt2 = [mod[l, :, k * d:(k + 1) * d].reshape(bsz, 1, d) for k in range(N_MOD)]
        w = w_in[l]
        w_main = jnp.concatenate(
            [w[:, :c_gk], w[:, c_gk + GLA_RANK:c_dt], w[:, c_dt + SSM_HEADS:]], axis=1).astype(BF16)
        pad = jnp.zeros((d, LANES - GLA_RANK - SSM_HEADS), F32)
        w_small = jnp.concatenate([w[:, c_gk:c_gk + GLA_RANK], w[:, c_dt:c_dt + SSM_HEADS], pad], axis=1)
        dt_bias_pad = jnp.concatenate([jnp.zeros((GLA_RANK,), F32), dt_bias[l], pad[0]]).reshape(1, LANES)
        proj, small = _norm_proj(xf, norm1_g[l].reshape(1, d), sc1, sh1, w_main, w_small, dt_bias_pad, seq)

        wg_pad = jnp.concatenate([gla_w_gk2[l], jnp.zeros((LANES - GLA_RANK, GLA_DK), F32)], axis=0)
        o_a = _gla(proj, small, wg_pad, gla_b_gk[l].reshape(1, GLA_DK),
                   gla_norm_g[l].reshape(1, GLA_HV), bsz, seq)
        o_b = _ssd(proj, small, conv_w[l], conv_b[l].reshape(1, -1),
                   jnp.repeat(a_log[l], SSM_HEADDIM).reshape(1, SSM_DINNER),
                   jnp.repeat(d_skip[l], SSM_HEADDIM).reshape(1, SSM_DINNER),
                   ssm_norm_g[l].reshape(1, SSM_DINNER), bsz, seq)

        w_r = jnp.concatenate([router_group_w[l], router_expert_w[l],
                               jnp.zeros((d, LANES - MOE_GROUPS - MOE_EXPERTS), F32)], axis=1)
        b_r = jnp.concatenate([router_group_b[l], router_expert_b[l],
                               jnp.zeros((LANES - MOE_GROUPS - MOE_EXPERTS,), F32)]).reshape(1, LANES)
        x1, h2, ri, rw, cnt = _merge(o_a, o_b, proj, xf, gt1, w_oa[l].astype(BF16), w_ob[l].astype(BF16),
                                     w_out[l].astype(BF16), norm2_g[l].reshape(1, d), sc2, sh2, w_r, b_r, seq)

        pos, ends, tile_expert, n_valid = _routing_tables(ri, cnt, n_tiles)
        xs = _dispatch(pos, ends, h2, n_rows)
        ys = _experts(tile_expert, n_valid, xs,
                      expert_w1.reshape(depth * MOE_EXPERTS, d, MOE_DFF),
                      expert_w3.reshape(depth * MOE_EXPERTS, d, MOE_DFF),
                      expert_w2.reshape(depth * MOE_EXPERTS, MOE_DFF, d), l)
        xf = _combine(pos, ys, x1, rw, gt2, final_norm_g.reshape(1, d), seq, final=(l == depth - 1))
    return xf.reshape(bsz, seq, d)
```

```python
import functools

import jax
import jax.numpy as jnp
from jax import lax
from jax.experimental import pallas as pl
from jax.experimental.pallas import tpu as pltpu

F32 = jnp.float32
BF16 = jnp.bfloat16

EPS = 1e-6
CHUNK = 64
D_MODEL = 1024
N_MOD = 6
GLA_HEADS = 4
GLA_HK = 128
GLA_HV = 256
GLA_DK = GLA_HEADS * GLA_HK
GLA_DV = GLA_HEADS * GLA_HV
GLA_RANK = 16
GLA_NORMALIZER = 16.0
SSM_DINNER = 2048
SSM_HEADDIM = 64
SSM_HEADS = SSM_DINNER // SSM_HEADDIM
SSM_GROUPS = 8
SSM_HPG = SSM_HEADS // SSM_GROUPS
SSM_GW = SSM_HPG * SSM_HEADDIM
SSM_DSTATE = 128
SSM_CONV = 4
MOE_GROUPS = 4
MOE_EPG = 8
MOE_EXPERTS = MOE_GROUPS * MOE_EPG
MOE_DFF = 256

OFF_Q = 0
OFF_K = OFF_Q + GLA_DK
OFF_V = OFF_K + GLA_DK
OFF_G = OFF_V + GLA_DV
OFF_Z = OFF_G + GLA_DV
OFF_XS = OFF_Z + SSM_DINNER
OFF_B = OFF_XS + SSM_DINNER
OFF_C = OFF_B + SSM_GROUPS * SSM_DSTATE
OFF_GA = OFF_C + SSM_GROUPS * SSM_DSTATE
OFF_GB = OFF_GA + D_MODEL
N_PROJ = OFF_GB + D_MODEL
LANES = 128
SM_DT = GLA_RANK
RL_E = MOE_GROUPS

V7X_VMEM_LIMIT = 56 * 1024 * 1024

TM_PROJ = 1024
NJ_PROJ = 4
TN_PROJ = N_PROJ // NJ_PROJ
TM_MERGE = 512
TM_EXPERT = 512
TM_COMBINE = 256
TM_POS = 2048
DISPATCH_BLK = 1024
MIX_BLK = 256
CPB = MIX_BLK // CHUNK
CONV_ROWS = 128
CONV_HALO = 16
CONV_WIN = CONV_HALO + CONV_ROWS


def _cparams(sem):
    return pltpu.CompilerParams(dimension_semantics=sem, vmem_limit_bytes=V7X_VMEM_LIMIT)


def _sig(x):
    return 0.5 * jnp.tanh(0.5 * x) + 0.5


def _softplus(x):
    return jnp.maximum(x, 0.0) + jnp.log(1.0 + jnp.exp(-jnp.abs(x)))


def _log_sigmoid(x):
    return jnp.minimum(x, 0.0) - jnp.log(1.0 + jnp.exp(-jnp.abs(x)))


def _split(a):
    hi = a.astype(BF16)
    lo = (a - hi.astype(F32)).astype(BF16)
    return hi, lo


def _mm(a, b):
    return jnp.dot(a, b, preferred_element_type=F32)


def _dot3(a, b):
    ah, al = _split(a)
    bh, bl = _split(b)
    return _mm(ah, bh) + _mm(al, bh) + _mm(ah, bl)


def _dot2_lhs(a, b_bf16):
    ah, al = _split(a)
    return _mm(ah, b_bf16) + _mm(al, b_bf16)


def _chunk_scan_matrix(n):
    r = lax.broadcasted_iota(jnp.int32, (n, n), 0)
    c = lax.broadcasted_iota(jnp.int32, (n, n), 1)
    return jnp.where((r // CHUNK == c // CHUNK) & (r >= c), 1.0, 0.0).astype(BF16)


def _chunk_scan(mat, x):
    n, w = x.shape
    hi, lo = _split(x)
    out = _mm(mat, jnp.concatenate([hi, lo], axis=1))
    cum = out[:, :w] + out[:, w:]
    last_rows = [cum[c * CHUNK + CHUNK - 1:(c + 1) * CHUNK, :] for c in range(n // CHUNK)]
    tot = jnp.concatenate([jnp.broadcast_to(row, (CHUNK, w)) for row in last_rows], axis=0)
    return cum, tot


def _rms(x):
    return x * lax.rsqrt(jnp.mean(x * x, axis=-1, keepdims=True) + EPS)


def _pack_halves(x):
    w = x.shape[1] // 2
    lo = lax.bitcast_convert_type(x[:, :w].astype(BF16).astype(F32), jnp.uint32)
    hi = lax.bitcast_convert_type(x[:, w:].astype(BF16).astype(F32), jnp.uint32)
    return (lo >> 16) | (hi & jnp.uint32(0xFFFF0000))


def _unpack_halves(p):
    lo = lax.bitcast_convert_type(p << 16, F32)
    hi = lax.bitcast_convert_type(p & jnp.uint32(0xFFFF0000), F32)
    return lo, hi


def _skewed_loop(n, stages):
    depth = len(stages)

    def step(i, lo, hi):
        for k in reversed(range(lo, hi)):
            stages[k](i - k)

    for i in range(depth - 1):
        step(i, 0, i + 1)

    def body(i, carry):
        step(i, 0, depth)
        return carry

    lax.fori_loop(depth - 1, n, body, 0)
    for i in range(n, n + depth - 1):
        step(i, i - n + 1, depth)


def _item_rows(i):
    start = i * MIX_BLK
    return start if isinstance(start, int) else pl.multiple_of(start, MIX_BLK)


def _mod_kernel(c_ref, w_ref, b_ref, o_ref):
    c = c_ref[...]
    o_ref[0] = _dot3(c * _sig(c), w_ref[0]) + b_ref[0]


def _modulation(c, ada_w, ada_b):
    depth, d, n = ada_w.shape
    bsz = c.shape[0]
    tn = 1536
    return pl.pallas_call(
        _mod_kernel,
        grid=(depth, n // tn),
        in_specs=[
            pl.BlockSpec((bsz, d), lambda l, j: (0, 0)),
            pl.BlockSpec((1, d, tn), lambda l, j: (l, 0, j)),
            pl.BlockSpec((1, 1, tn), lambda l, j: (l, 0, j)),
        ],
        out_specs=pl.BlockSpec((1, bsz, tn), lambda l, j: (l, 0, j)),
        out_shape=jax.ShapeDtypeStruct((depth, bsz, n), F32),
        compiler_params=_cparams(("arbitrary", "arbitrary")),
        name="adaln_mod",
    )(c, ada_w, ada_b.reshape(depth, 1, n))


W_UNIT = 256
W_GK_UNIT = (OFF_G + GLA_DV) // W_UNIT
W_DT_UNIT = OFF_GA // W_UNIT


def _w_in_kernel(a_ref, b_ref, wm_ref, ws_ref):
    u = pl.program_id(1)
    win = jnp.concatenate([a_ref[0], b_ref[0]], axis=1)
    cut = jnp.where(u < W_GK_UNIT, 0, jnp.where(u < W_DT_UNIT, GLA_RANK, GLA_RANK + SSM_HEADS))
    wm_ref[0] = pltpu.roll(win, win.shape[1] - cut, 1)[:, :W_UNIT].astype(BF16)

    lane = lax.broadcasted_iota(jnp.int32, (win.shape[0], LANES), 1)
    head = win[:, :LANES]

    @pl.when(u == W_GK_UNIT)
    def _():
        ws_ref[0] = jnp.where(lane < GLA_RANK, head, 0.0)

    @pl.when(u == W_DT_UNIT)
    def _():
        ws_ref[0] = jnp.where((lane >= SM_DT) & (lane < SM_DT + SSM_HEADS), head, ws_ref[0])


def _prep_w_in(w_in):
    depth, d, _ = w_in.shape
    return pl.pallas_call(
        _w_in_kernel,
        grid=(depth, N_PROJ // W_UNIT),
        in_specs=[
            pl.BlockSpec((1, d, W_UNIT), lambda l, u: (l, 0, u)),
            pl.BlockSpec((1, d, LANES), lambda l, u: (l, 0, (W_UNIT // LANES) * (u + 1))),
        ],
        out_specs=[
            pl.BlockSpec((1, d, W_UNIT), lambda l, u: (l, 0, u)),
            pl.BlockSpec((1, d, LANES), lambda l, u: (l, 0, 0)),
        ],
        out_shape=[
            jax.ShapeDtypeStruct((depth, d, N_PROJ), BF16),
            jax.ShapeDtypeStruct((depth, d, LANES), F32),
        ],
        compiler_params=_cparams(("arbitrary", "arbitrary")),
        name="w_in_layout",
    )(w_in, w_in)


def _norm_proj_kernel(x0_ref, xn_ref, g_ref, sc0_ref, sh0_ref, scn_ref, shn_ref, w_ref, ws_ref, dtb_ref,
                      o_ref, os_ref, h_s, side_s):
    i = pl.program_id(0)
    j = pl.program_id(1)
    cur = i % 2
    rows_per_step = TM_PROJ // NJ_PROJ

    def norm_rows(x_ref, sc_ref, sh_ref, rows, slot):
        h = _rms(x_ref[rows, :]) * g_ref[...] * (1.0 + sc_ref[0]) + sh_ref[0]
        h_s[slot, rows, :] = h.astype(BF16)
        side = _dot3(h, ws_ref[...])
        lane = lax.broadcasted_iota(jnp.int32, side.shape, 1)
        is_dt = (lane >= SM_DT) & (lane < SM_DT + SSM_HEADS)
        side_s[slot, rows, :] = jnp.where(is_dt, _softplus(side + dtb_ref[...]), side)

    @pl.when((i == 0) & (j == 0))
    def _():
        for r in range(NJ_PROJ):
            norm_rows(x0_ref, sc0_ref, sh0_ref, pl.ds(r * rows_per_step, rows_per_step), 0)

    norm_rows(xn_ref, scn_ref, shn_ref,
              pl.ds(pl.multiple_of(j * rows_per_step, rows_per_step), rows_per_step), 1 - cur)

    @pl.when(j == 0)
    def _():
        os_ref[...] = side_s[cur]

    o_ref[...] = _mm(h_s[cur], w_ref[...]).astype(BF16)


def _norm_proj(x, g, sc, sh, w_main, w_small, dt_bias_pad, seq, layer):
    t, d = x.shape
    tpb = seq // TM_PROJ
    last = t // TM_PROJ - 1
    nxt = lambda i: jnp.minimum(i + 1, last)
    return pl.pallas_call(
        _norm_proj_kernel,
        grid=(t // TM_PROJ, NJ_PROJ),
        in_specs=[
            pl.BlockSpec((TM_PROJ, d), lambda i, j: (0, 0)),
            pl.BlockSpec((TM_PROJ, d), lambda i, j: (nxt(i), 0)),
            pl.BlockSpec((1, d), lambda i, j: (0, 0)),
            pl.BlockSpec((1, 1, d), lambda i, j: (0, 0, 0)),
            pl.BlockSpec((1, 1, d), lambda i, j: (0, 0, 0)),
            pl.BlockSpec((1, 1, d), lambda i, j: (nxt(i) // tpb, 0, 0)),
            pl.BlockSpec((1, 1, d), lambda i, j: (nxt(i) // tpb, 0, 0)),
            pl.BlockSpec((None, d, TN_PROJ), lambda i, j: (layer, 0, j)),
            pl.BlockSpec((None, d, LANES), lambda i, j: (layer, 0, 0)),
            pl.BlockSpec((1, LANES), lambda i, j: (0, 0)),
        ],
        out_specs=[
            pl.BlockSpec((TM_PROJ, TN_PROJ), lambda i, j: (i, j)),
            pl.BlockSpec((TM_PROJ, LANES), lambda i, j: (i, 0)),
        ],
        out_shape=[
            jax.ShapeDtypeStruct((t, N_PROJ), BF16),
            jax.ShapeDtypeStruct((t, LANES), F32),
        ],
        scratch_shapes=[pltpu.VMEM((2, TM_PROJ, d), BF16), pltpu.VMEM((2, TM_PROJ, LANES), F32)],
        compiler_params=_cparams(("arbitrary", "arbitrary")),
        name="norm_in_proj",
    )(x, x, g, sc, sh, sc, sh, w_main, w_small, dt_bias_pad)


def _gla_kernel(q_ref, k_ref, v_ref, g_ref, sm_ref, wg_ref, bg_ref, ng_ref, o_ref,
                gk_s, kdec_s, edec_s, upd_s, st_s, oraw_s):
    seq = q_ref.shape[0]
    scan = _chunk_scan_matrix(MIX_BLK)
    wg_hi, wg_lo = _split(wg_ref[...])
    bg = bg_ref[...]
    ng = ng_ref[...]
    scale = GLA_HK ** -0.5
    st_s[...] = jnp.zeros_like(st_s)

    def stage_gate(i):
        rows = pl.ds(_item_rows(i), MIX_BLK)
        sm_hi, sm_lo = _split(sm_ref[rows, :])
        pre = _mm(sm_hi, wg_hi) + _mm(sm_lo, wg_hi) + _mm(sm_hi, wg_lo) + bg
        gk_s[rows, :] = _log_sigmoid(pre) * (1.0 / GLA_NORMALIZER)

    def stage_decay(i):
        rows = pl.ds(_item_rows(i), MIX_BLK)
        cum, tot = _chunk_scan(scan, gk_s[rows, :])
        kdec_s[rows, :] = (k_ref[rows, :].astype(F32) * jnp.exp(tot - cum)).astype(BF16)
        edec_s[rows, :] = jnp.exp(tot)

    def stage_update(i):
        for c in range(CPB):
            rows = pl.ds(_item_rows(i) + c * CHUNK, CHUNK)
            upd_s[i * CPB + c] = lax.dot_general(
                v_ref[rows, :], kdec_s[rows, :], (((0,), (0,)), ((), ())), preferred_element_type=F32)

    def stage_state(i):
        st = st_s[...]
        for c in range(CPB):
            r0 = _item_rows(i) + c * CHUNK
            st = edec_s[pl.ds(r0, 1), :] * st + upd_s[i * CPB + c]
            oraw_s[pl.ds(r0, CHUNK), :] = lax.dot_general(
                q_ref[pl.ds(r0, CHUNK), :], st.astype(BF16), (((1,), (1,)), ((), ())),
                preferred_element_type=F32)
        st_s[...] = st

    def stage_out(i):
        rows = pl.ds(_item_rows(i), MIX_BLK)
        o = _rms(oraw_s[rows, :] * scale) * ng
        g = g_ref[rows, :].astype(F32)
        o_ref[rows, :] = (o * (g * _sig(g))).astype(BF16)

    _skewed_loop(seq // MIX_BLK, [stage_gate, stage_decay, stage_update, stage_state, stage_out])


def _gla(proj, small, wg_pad, b_gk, norm_g, bsz, seq):
    t = proj.shape[0]
    kq, kk = OFF_Q // GLA_HK, OFF_K // GLA_HK
    kv, kg = OFF_V // GLA_HV, OFF_G // GLA_HV
    return pl.pallas_call(
        _gla_kernel,
        grid=(bsz, GLA_HEADS),
        in_specs=[
            pl.BlockSpec((seq, GLA_HK), lambda b, h: (b, kq + h)),
            pl.BlockSpec((seq, GLA_HK), lambda b, h: (b, kk + h)),
            pl.BlockSpec((seq, GLA_HV), lambda b, h: (b, kv + h)),
            pl.BlockSpec((seq, GLA_HV), lambda b, h: (b, kg + h)),
            pl.BlockSpec((seq, LANES), lambda b, h: (b, 0)),
            pl.BlockSpec((LANES, GLA_HK), lambda b, h: (0, h)),
            pl.BlockSpec((1, GLA_HK), lambda b, h: (0, h)),
            pl.BlockSpec((1, GLA_HV), lambda b, h: (0, 0)),
        ],
        out_specs=pl.BlockSpec((seq, GLA_HV), lambda b, h: (b, h)),
        out_shape=jax.ShapeDtypeStruct((t, GLA_DV), BF16),
        scratch_shapes=[
            pltpu.VMEM((seq, GLA_HK), F32),
            pltpu.VMEM((seq, GLA_HK), BF16),
            pltpu.VMEM((seq, GLA_HK), F32),
            pltpu.VMEM((seq // CHUNK, GLA_HV, GLA_HK), F32),
            pltpu.VMEM((GLA_HV, GLA_HK), F32),
            pltpu.VMEM((seq, GLA_HV), F32),
        ],
        compiler_params=_cparams(("arbitrary", "arbitrary")),
        name="gla_mixer",
    )(proj, proj, proj, proj, small, wg_pad, b_gk, norm_g)


def _conv_shift_matrix():
    t = lax.broadcasted_iota(jnp.int32, (CONV_ROWS, SSM_CONV * CONV_WIN), 0)
    c = lax.broadcasted_iota(jnp.int32, (CONV_ROWS, SSM_CONV * CONV_WIN), 1)
    return jnp.where(c % CONV_WIN == t + CONV_HALO - c // CONV_WIN, 1.0, 0.0).astype(BF16)


def _conv_window(ref, r0):
    if isinstance(r0, int) and r0 == 0:
        return jnp.concatenate([jnp.zeros((CONV_HALO, ref.shape[1]), BF16), ref[0:CONV_ROWS, :]], axis=0)
    start = r0 - CONV_HALO
    return ref[pl.ds(start if isinstance(start, int) else pl.multiple_of(start, CONV_HALO), CONV_WIN), :]


def _ssd_kernel(z_ref, xs_ref, b_ref, c_ref, sm_ref, cwx_ref, cwb_ref, cwc_ref, cbx_ref, cbb_ref,
                cbc_ref, alog_ref, dsk_ref, ng_ref, o_ref,
                xs_s, b_s, c_s, dt_s, a_s, ea_s, edec_s, xdt_s, decx_s, m_s, upd_s, st_s, y_s):
    seq = z_ref.shape[0]
    grp = pl.program_id(1)
    er = lax.broadcasted_iota(jnp.int32, (LANES, SSM_GW), 0)
    el = lax.broadcasted_iota(jnp.int32, (LANES, SSM_GW), 1)
    expand = jnp.where(er == SM_DT + SSM_HPG * grp + el // SSM_HEADDIM, 1.0, 0.0).astype(BF16)
    scan = _chunk_scan_matrix(MIX_BLK)
    a_row = -jnp.exp(alog_ref[...])
    dsk = dsk_ref[...]
    ng = ng_ref[...]
    dr = lax.broadcasted_iota(jnp.int32, (CHUNK, SSM_GW), 0)
    dl = lax.broadcasted_iota(jnp.int32, (CHUNK, SSM_GW), 1)
    diag = jnp.where(dr == dl % CHUNK, 1.0, 0.0)
    br = lax.broadcasted_iota(jnp.int32, (SSM_GW, SSM_GW), 0)
    bl = lax.broadcasted_iota(jnp.int32, (SSM_GW, SSM_GW), 1)
    blockmask = jnp.where(br // SSM_HEADDIM == bl // SSM_HEADDIM, 1.0, 0.0).astype(BF16)
    st_s[...] = jnp.zeros_like(st_s)
    shift = _conv_shift_matrix()
    taps = jnp.concatenate([cwx_ref[...], cwb_ref[...], cwc_ref[...]], axis=1)
    conv_bias = jnp.concatenate([cbx_ref[...], cbb_ref[...], cbc_ref[...]], axis=1)

    def stage_conv(i):
        rows = pl.ds(_item_rows(i), MIX_BLK)
        dt_s[rows, :] = _dot2_lhs(sm_ref[rows, :], expand)
        for sub in range(MIX_BLK // CONV_ROWS):
            r0 = _item_rows(i) + sub * CONV_ROWS
            win = jnp.concatenate(
                [_conv_window(xs_ref, r0), _conv_window(b_ref, r0), _conv_window(c_ref, r0)], axis=1)
            scaled = jnp.concatenate(
                [win * taps[SSM_CONV - 1 - k:SSM_CONV - k, :].astype(BF16) for k in range(SSM_CONV)], axis=0)
            acc = _mm(shift, scaled) + conv_bias
            act = acc * _sig(acc)
            out_rows = pl.ds(r0, CONV_ROWS)
            xs_s[out_rows, :] = act[:, :SSM_GW]
            b_s[out_rows, :] = act[:, SSM_GW:SSM_GW + SSM_DSTATE].astype(BF16)
            c_s[out_rows, :] = act[:, SSM_GW + SSM_DSTATE:].astype(BF16)

    def stage_decay(i):
        rows = pl.ds(_item_rows(i), MIX_BLK)
        dt = dt_s[rows, :]
        acum, atot = _chunk_scan(scan, dt * a_row)
        a_s[rows, :] = acum
        ea_s[rows, :] = jnp.exp(acum)
        edec_s[rows, :] = jnp.exp(atot)
        xdt = xs_s[rows, :] * dt
        xdt_s[rows, :] = xdt.astype(BF16)
        decx_s[rows, :] = (xdt * jnp.exp(atot - acum)).astype(BF16)

    def stage_intra(i):
        for c in range(CPB):
            rows = pl.ds(_item_rows(i) + c * CHUNK, CHUNK)
            acum = a_s[rows, :]
            acum_s = jnp.sum(acum * diag, axis=0, keepdims=True)
            lmask = jnp.exp(-jnp.abs(acum - acum_s))
            bm = b_s[rows, :]
            cb = lax.dot_general(c_s[rows, :], jnp.concatenate([bm] * SSM_HPG, axis=0),
                                 (((1,), (1,)), ((), ())), preferred_element_type=F32)
            m_s[rows, :] = (cb * lmask).astype(BF16)
            upd_s[i * CPB + c] = lax.dot_general(bm, decx_s[rows, :], (((0,), (0,)), ((), ())),
                                                 preferred_element_type=F32)

    def stage_state(i):
        st = st_s[...]
        for c in range(CPB):
            r0 = _item_rows(i) + c * CHUNK
            rows = pl.ds(r0, CHUNK)
            xdt_bd = jnp.concatenate([xdt_s[rows, :]] * SSM_HPG, axis=0) * blockmask
            y = _mm(m_s[rows, :], xdt_bd) + _mm(c_s[rows, :], st.astype(BF16)) * ea_s[rows, :]
            st = edec_s[pl.ds(r0, 1), :] * st + upd_s[i * CPB + c]
            y = y + xs_s[rows, :] * dsk
            z = z_ref[rows, :].astype(F32)
            y_s[rows, :] = y * (z * _sig(z))
        st_s[...] = st

    def stage_out(i):
        rows = pl.ds(_item_rows(i), MIX_BLK)
        o_ref[rows, :] = (_rms(y_s[rows, :]) * ng).astype(BF16)

    _skewed_loop(seq // MIX_BLK, [stage_conv, stage_decay, stage_intra, stage_state, stage_out])


def _ssd(proj, small, conv_w, conv_b, a_log_e, d_skip_e, norm_g, bsz, seq):
    t = proj.shape[0]
    kz, kx = OFF_Z // SSM_GW, OFF_XS // SSM_GW
    kb, kc = OFF_B // SSM_DSTATE, OFF_C // SSM_DSTATE
    nb = SSM_DINNER // SSM_DSTATE
    ncg = SSM_GROUPS
    wide_f32 = pltpu.VMEM((seq, SSM_GW), F32)
    wide_bf16 = pltpu.VMEM((seq, SSM_GW), BF16)
    return pl.pallas_call(
        _ssd_kernel,
        grid=(bsz, SSM_GROUPS),
        in_specs=[
            pl.BlockSpec((seq, SSM_GW), lambda b, g: (b, kz + g)),
            pl.BlockSpec((seq, SSM_GW), lambda b, g: (b, kx + g)),
            pl.BlockSpec((seq, SSM_DSTATE), lambda b, g: (b, kb + g)),
            pl.BlockSpec((seq, SSM_DSTATE), lambda b, g: (b, kc + g)),
            pl.BlockSpec((seq, LANES), lambda b, g: (b, 0)),
            pl.BlockSpec((SSM_CONV, SSM_GW), lambda b, g: (0, g)),
            pl.BlockSpec((SSM_CONV, SSM_DSTATE), lambda b, g: (0, nb + g)),
            pl.BlockSpec((SSM_CONV, SSM_DSTATE), lambda b, g: (0, nb + ncg + g)),
            pl.BlockSpec((1, SSM_GW), lambda b, g: (0, g)),
            pl.BlockSpec((1, SSM_DSTATE), lambda b, g: (0, nb + g)),
            pl.BlockSpec((1, SSM_DSTATE), lambda b, g: (0, nb + ncg + g)),
            pl.BlockSpec((1, SSM_GW), lambda b, g: (0, g)),
            pl.BlockSpec((1, SSM_GW), lambda b, g: (0, g)),
            pl.BlockSpec((1, SSM_GW), lambda b, g: (0, g)),
        ],
        out_specs=pl.BlockSpec((seq, SSM_GW), lambda b, g: (b, g)),
        out_shape=jax.ShapeDtypeStruct((t, SSM_DINNER), BF16),
        scratch_shapes=[
            wide_f32,
            pltpu.VMEM((seq, SSM_DSTATE), BF16),
            pltpu.VMEM((seq, SSM_DSTATE), BF16),
            wide_f32,
            wide_f32,
            wide_f32,
            wide_f32,
            wide_bf16,
            wide_bf16,
            wide_bf16,
            pltpu.VMEM((seq // CHUNK, SSM_DSTATE, SSM_GW), F32),
            pltpu.VMEM((SSM_DSTATE, SSM_GW), F32),
            wide_f32,
        ],
        compiler_params=_cparams(("arbitrary", "arbitrary")),
        name="ssd_mixer",
    )(proj, proj, proj, proj, small, conv_w, conv_w, conv_w, conv_b, conv_b, conv_b,
      a_log_e, d_skip_e, norm_g)


def _lane_min(x):
    return jnp.min(x, axis=-1, keepdims=True)


def _lane_max(x):
    return jnp.max(x, axis=-1, keepdims=True)


def _merge_kernel(oa_ref, ob_ref, ga_ref, gb_ref, x_ref, gt_ref, woa_ref, wob_ref, wout_ref,
                  g2_ref, sc_ref, sh_ref, wr_ref, br_ref,
                  x1_ref, h2_ref, ri_ref, rw_ref, cnt_ref, cnt_s, h_s):
    i = pl.program_id(0)
    tm = x_ref.shape[0]

    @pl.when(i == 0)
    def _():
        cnt_s[...] = jnp.zeros_like(cnt_s)
        h_s[...] = jnp.zeros_like(h_s)

    logit = _dot3(h_s[...], wr_ref[...]) + br_ref[...]

    ya = _mm(oa_ref[...], woa_ref[...])
    yb = _mm(ob_ref[...], wob_ref[...])

    lane = lax.broadcasted_iota(jnp.int32, (tm, LANES), 1)
    lane_f = lane.astype(F32)
    neg = jnp.float32(-jnp.inf)
    big = jnp.float32(LANES)
    is_g = lane < MOE_GROUPS
    lg = jnp.where(is_g, logit, neg)
    gmax = _lane_max(lg)
    gsum = jnp.sum(jnp.where(is_g, jnp.exp(lg - gmax), 0.0), axis=-1, keepdims=True)
    gidx = _lane_min(jnp.where(lg == gmax, lane_f, big))
    lo = RL_E + MOE_EPG * gidx
    in_grp = (lane_f >= lo) & (lane_f < lo + MOE_EPG)
    le = jnp.where(in_grp, logit, neg)
    emax = _lane_max(le)
    l0 = _lane_min(jnp.where(le == emax, lane_f, big))
    le2 = jnp.where(lane_f == l0, neg, le)
    emax2 = _lane_max(le2)
    l1 = _lane_min(jnp.where(le2 == emax2, lane_f, big))
    ratio = jnp.exp(emax2 - emax)
    w0 = 1.0 / (gsum * (1.0 + ratio))
    w1 = w0 * ratio

    counted = jnp.where(i > 0, 1.0, 0.0)
    oh0 = lane_f == l0
    oh1 = lane_f == l1
    oh = jnp.where(oh0 | oh1, counted, 0.0)
    tr = lax.broadcasted_iota(jnp.int32, (tm, tm), 0)
    tc = lax.broadcasted_iota(jnp.int32, (tm, tm), 1)
    before = _mm(jnp.where(tr > tc, 1.0, 0.0).astype(BF16), oh.astype(BF16)) + cnt_s[0:1, :]
    r0 = jnp.sum(jnp.where(oh0, before, 0.0), axis=-1, keepdims=True)
    r1 = jnp.sum(jnp.where(oh1, before, 0.0), axis=-1, keepdims=True)
    total = cnt_s[0:1, :] + jnp.sum(oh, axis=0, keepdims=True)
    cnt_s[...] = jnp.broadcast_to(total, cnt_s.shape)
    cnt_ref[...] = jnp.broadcast_to(total, cnt_ref.shape)

    ri = jnp.where(lane == 0, l0 - RL_E,
                   jnp.where(lane == 1, l1 - RL_E, jnp.where(lane == 2, r0, jnp.where(lane == 3, r1, 0.0))))
    ri_ref[...] = ri.astype(jnp.int32)
    rw_ref[...] = jnp.where(lane == 0, w0, jnp.where(lane == 1, w1, 0.0))

    merged = _sig(ga_ref[...].astype(F32)) * ya + _sig(gb_ref[...].astype(F32)) * yb
    x1 = x_ref[...] + gt_ref[0] * _mm(merged.astype(BF16), wout_ref[...])
    x1_ref[...] = x1
    h = _rms(x1) * g2_ref[...] * (1.0 + sc_ref[0]) + sh_ref[0]
    h2_ref[...] = _pack_halves(h)
    h_s[...] = h


def _merge(o_a, o_b, proj, x, gt1, w_oa, w_ob, w_out, g2, sc2, sh2, w_r, b_r, seq):
    t, d = x.shape
    tm = TM_MERGE
    tpb = seq // tm
    n = t // tm
    kga, kgb = OFF_GA // d, OFF_GB // d
    const = lambda i: (0, 0)
    cur = lambda i: jnp.minimum(i, n - 1)
    prev = lambda i: jnp.maximum(i - 1, 0)
    return pl.pallas_call(
        _merge_kernel,
        grid=(n + 1,),
        in_specs=[
            pl.BlockSpec((tm, GLA_DV), lambda i: (cur(i), 0)),
            pl.BlockSpec((tm, SSM_DINNER), lambda i: (cur(i), 0)),
            pl.BlockSpec((tm, d), lambda i: (cur(i), kga)),
            pl.BlockSpec((tm, d), lambda i: (cur(i), kgb)),
            pl.BlockSpec((tm, d), lambda i: (cur(i), 0)),
            pl.BlockSpec((1, 1, d), lambda i: (cur(i) // tpb, 0, 0)),
            pl.BlockSpec((GLA_DV, d), const),
            pl.BlockSpec((SSM_DINNER, d), const),
            pl.BlockSpec((d, d), const),
            pl.BlockSpec((1, d), const),
            pl.BlockSpec((1, 1, d), lambda i: (cur(i) // tpb, 0, 0)),
            pl.BlockSpec((1, 1, d), lambda i: (cur(i) // tpb, 0, 0)),
            pl.BlockSpec((d, LANES), const),
            pl.BlockSpec((1, LANES), const),
        ],
        out_specs=[
            pl.BlockSpec((tm, d), lambda i: (cur(i), 0)),
            pl.BlockSpec((tm, d // 2), lambda i: (cur(i), 0)),
            pl.BlockSpec((tm, LANES), lambda i: (prev(i), 0)),
            pl.BlockSpec((tm, LANES), lambda i: (prev(i), 0)),
            pl.BlockSpec((8, LANES), const),
        ],
        out_shape=[
            jax.ShapeDtypeStruct((t, d), F32),
            jax.ShapeDtypeStruct((t, d // 2), jnp.uint32),
            jax.ShapeDtypeStruct((t, LANES), jnp.int32),
            jax.ShapeDtypeStruct((t, LANES), F32),
            jax.ShapeDtypeStruct((8, LANES), F32),
        ],
        scratch_shapes=[pltpu.VMEM((8, LANES), F32), pltpu.VMEM((tm, d), F32)],
        compiler_params=_cparams(("arbitrary",)),
        name="merge_router",
    )(o_a, o_b, proj, proj, x, gt1, w_oa, w_ob, w_out, g2, sc2, sh2, w_r, b_r)


def _pos_kernel(ri_ref, st_ref, pos_ref):
    ri = ri_ref[...].astype(F32)
    lane = lax.broadcasted_iota(jnp.int32, ri.shape, 1)
    lane_f = lane.astype(F32)
    starts = st_ref[...]
    p0 = jnp.sum(jnp.where(lane_f == ri[:, 0:1], starts, 0.0), axis=-1, keepdims=True) + ri[:, 2:3]
    p1 = jnp.sum(jnp.where(lane_f == ri[:, 1:2], starts, 0.0), axis=-1, keepdims=True) + ri[:, 3:4]
    pos_ref[...] = jnp.where(lane == 0, p0, jnp.where(lane == 1, p1, 0.0)).astype(jnp.int32)


def _positions(ri, starts_lanes):
    t = ri.shape[0]
    return pl.pallas_call(
        _pos_kernel,
        grid=(t // TM_POS,),
        in_specs=[
            pl.BlockSpec((TM_POS, LANES), lambda i: (i, 0)),
            pl.BlockSpec((1, LANES), lambda i: (0, 0)),
        ],
        out_specs=pl.BlockSpec((TM_POS, LANES), lambda i: (i, 0)),
        out_shape=jax.ShapeDtypeStruct((t, LANES), jnp.int32),
        compiler_params=_cparams(("arbitrary",)),
        name="moe_positions",
    )(ri, starts_lanes)


def _row_copy(src_ref, src_row, dst_ref, dst_row, sem):
    return pltpu.make_async_copy(src_ref.at[pl.ds(src_row, 1)], dst_ref.at[pl.ds(dst_row, 1)], sem)


def _dispatch_kernel(pos_ref, ends_ref, h_ref, xs_ref, zero_s, sem, zsem):
    n_tok = h_ref.shape[0]

    @pl.when(pl.program_id(0) == 0)
    def _():
        zero_s[...] = jnp.zeros_like(zero_s)

        def last_tile(e):
            end = ends_ref[e]
            begin = jnp.where(e > 0, ends_ref[jnp.maximum(e - 1, 0)], 0)
            dst = xs_ref.at[pl.ds(pl.multiple_of(jnp.maximum(end - TM_EXPERT, 0), TM_EXPERT), TM_EXPERT)]
            return end > begin, pltpu.make_async_copy(zero_s, dst, zsem)

        def fill(e, carry):
            nonempty, cp = last_tile(e)

            @pl.when(nonempty)
            def _():
                cp.start()

            return carry

        def fill_done(e, carry):
            nonempty, cp = last_tile(e)

            @pl.when(nonempty)
            def _():
                cp.wait()

            return carry

        n_used = ends_ref[MOE_EXPERTS - 1] // TM_EXPERT

        def unused_tile(tile):
            dst = xs_ref.at[pl.ds(pl.multiple_of(tile * TM_EXPERT, TM_EXPERT), TM_EXPERT)]
            return pltpu.make_async_copy(zero_s, dst, zsem)

        def tail(tile, carry):
            unused_tile(tile).start()
            return carry

        def tail_done(tile, carry):
            unused_tile(tile).wait()
            return carry

        n_tiles = xs_ref.shape[0] // TM_EXPERT
        lax.fori_loop(0, MOE_EXPERTS, fill, 0)
        lax.fori_loop(n_used, n_tiles, tail, 0)
        lax.fori_loop(0, MOE_EXPERTS, fill_done, 0)
        lax.fori_loop(n_used, n_tiles, tail_done, 0)

    def issue(tok, carry):
        _row_copy(h_ref, tok, xs_ref, pos_ref[2 * tok], sem).start(priority=0)
        _row_copy(h_ref, tok, xs_ref, pos_ref[2 * tok + 1], sem).start(priority=1)
        return carry

    lax.fori_loop(0, n_tok, issue, 0, unroll=8)

    def drain(tok, carry):
        _row_copy(h_ref, 0, xs_ref, 0, sem).wait()
        _row_copy(h_ref, 0, xs_ref, 0, sem).wait()
        return carry

    lax.fori_loop(0, n_tok, drain, 0, unroll=8)


def _dispatch(pos_flat, ends, h2, n_rows):
    t, d = h2.shape
    return pl.pallas_call(
        _dispatch_kernel,
        grid=(t // DISPATCH_BLK,),
        in_specs=[
            pl.BlockSpec((2 * DISPATCH_BLK,), lambda i: (i,), memory_space=pltpu.SMEM),
            pl.BlockSpec(memory_space=pltpu.SMEM),
            pl.BlockSpec((DISPATCH_BLK, d), lambda i: (i, 0)),
        ],
        out_specs=pl.BlockSpec(memory_space=pl.ANY),
        out_shape=jax.ShapeDtypeStruct((n_rows, d), h2.dtype),
        scratch_shapes=[pltpu.VMEM((TM_EXPERT, d), h2.dtype), pltpu.SemaphoreType.DMA(()),
                        pltpu.SemaphoreType.DMA(())],
        compiler_params=_cparams(("arbitrary",)),
        name="moe_dispatch",
    )(pos_flat, ends, h2)


def _expert_kernel(te_ref, nv_ref, x_ref, w1_ref, w3_ref, w2_ref, o_ref, w13_s, w2_s):
    i = pl.program_id(0)
    valid = i < nv_ref[0]
    new_expert = (i == 0) | (te_ref[i] != te_ref[jnp.maximum(i - 1, 0)])

    @pl.when(valid & new_expert)
    def _():
        w13_s[:, :MOE_DFF] = w1_ref[0].astype(BF16)
        w13_s[:, MOE_DFF:] = w3_ref[0].astype(BF16)
        w2_s[...] = w2_ref[0].astype(BF16)

    @pl.when(valid)
    def _():
        half = w13_s.shape[0] // 2
        x_lo, x_hi = _unpack_halves(x_ref[...])
        ab = _mm(x_lo.astype(BF16), w13_s[:half, :]) + _mm(x_hi.astype(BF16), w13_s[half:, :])
        a = ab[:, :MOE_DFF]
        o_ref[...] = _pack_halves(_mm((a * _sig(a) * ab[:, MOE_DFF:]).astype(BF16), w2_s[...]))

    @pl.when(jnp.logical_not(valid))
    def _():
        o_ref[...] = jnp.zeros_like(o_ref)


def _experts(tile_expert, n_valid, xs, w1, w3, w2, layer):
    n_rows, dp = xs.shape
    d = 2 * dp
    tm = TM_EXPERT
    first = layer * MOE_EXPERTS
    grid_spec = pltpu.PrefetchScalarGridSpec(
        num_scalar_prefetch=2,
        grid=(n_rows // tm,),
        in_specs=[
            pl.BlockSpec((tm, dp), lambda i, te, nv: (jnp.minimum(i, nv[0] - 1), 0)),
            pl.BlockSpec((1, d, MOE_DFF), lambda i, te, nv: (first + te[i], 0, 0)),
            pl.BlockSpec((1, d, MOE_DFF), lambda i, te, nv: (first + te[i], 0, 0)),
            pl.BlockSpec((1, MOE_DFF, d), lambda i, te, nv: (first + te[i], 0, 0)),
        ],
        out_specs=pl.BlockSpec((tm, dp), lambda i, te, nv: (i, 0)),
        scratch_shapes=[pltpu.VMEM((d, 2 * MOE_DFF), BF16), pltpu.VMEM((MOE_DFF, d), BF16)],
    )
    return pl.pallas_call(
        _expert_kernel,
        grid_spec=grid_spec,
        out_shape=jax.ShapeDtypeStruct((n_rows, dp), jnp.uint32),
        compiler_params=_cparams(("arbitrary",)),
        name="moe_experts",
    )(tile_expert, n_valid, xs, w1, w3, w2)


def _combine_kernel(pos_ref, pos_next_ref, ys_ref, x_ref, rw_ref, gt_ref, fg_ref, o_ref, buf, sem, *, final):
    tm = x_ref.shape[0]
    i = pl.program_id(0)
    slot = i % 2

    def gather(p_ref, s):
        def issue(tok, carry):
            _row_copy(ys_ref, p_ref[2 * tok], buf.at[s, 0], tok, sem.at[s]).start(priority=0)
            _row_copy(ys_ref, p_ref[2 * tok + 1], buf.at[s, 1], tok, sem.at[s]).start(priority=1)
            return carry

        lax.fori_loop(0, tm, issue, 0, unroll=8)

    @pl.when(i == 0)
    def _():
        gather(pos_ref, 0)

    @pl.when(i + 1 < pl.num_programs(0))
    def _():
        gather(pos_next_ref, 1 - slot)

    def drain(tok, carry):
        _row_copy(ys_ref, 0, buf.at[slot, 0], 0, sem.at[slot]).wait()
        _row_copy(ys_ref, 0, buf.at[slot, 1], 0, sem.at[slot]).wait()
        return carry

    lax.fori_loop(0, tm, drain, 0, unroll=8)

    rw = rw_ref[...]
    w0, w1 = rw[:, 0:1], rw[:, 1:2]
    y0_lo, y0_hi = _unpack_halves(buf[slot, 0])
    y1_lo, y1_hi = _unpack_halves(buf[slot, 1])
    y = jnp.concatenate([w0 * y0_lo + w1 * y1_lo, w0 * y0_hi + w1 * y1_hi], axis=1)
    out = x_ref[...] + gt_ref[0] * y
    if final:
        out = _rms(out) * fg_ref[...]
    o_ref[...] = out


def _combine(pos_flat, ys, x1, rw, gt2, final_g, seq, final):
    t, d = x1.shape
    tm = TM_COMBINE
    tpb = seq // tm
    last = t // tm - 1
    return pl.pallas_call(
        functools.partial(_combine_kernel, final=final),
        grid=(t // tm,),
        in_specs=[
            pl.BlockSpec((2 * tm,), lambda i: (i,), memory_space=pltpu.SMEM),
            pl.BlockSpec((2 * tm,), lambda i: (jnp.minimum(i + 1, last),), memory_space=pltpu.SMEM),
            pl.BlockSpec(memory_space=pl.ANY),
            pl.BlockSpec((tm, d), lambda i: (i, 0)),
            pl.BlockSpec((tm, LANES), lambda i: (i, 0)),
            pl.BlockSpec((1, 1, d), lambda i: (i // tpb, 0, 0)),
            pl.BlockSpec((1, d), lambda i: (0, 0)),
        ],
        out_specs=pl.BlockSpec((tm, d), lambda i: (i, 0)),
        out_shape=jax.ShapeDtypeStruct((t, d), F32),
        scratch_shapes=[pltpu.VMEM((2, 2, tm, d // 2), jnp.uint32), pltpu.SemaphoreType.DMA((2,))],
        compiler_params=_cparams(("arbitrary",)),
        name="moe_combine",
    )(pos_flat, pos_flat, ys, x1, rw, gt2, final_g)


def _routing_tables(ri, cnt, n_tiles):
    counts = cnt[0, RL_E:RL_E + MOE_EXPERTS].astype(jnp.int32)
    padded = ((counts + TM_EXPERT - 1) // TM_EXPERT) * TM_EXPERT
    ends = jnp.cumsum(padded)
    starts = ends - padded
    starts_lanes = jnp.concatenate(
        [starts.astype(F32), jnp.zeros((LANES - MOE_EXPERTS,), F32)]).reshape(1, LANES)
    pos = _positions(ri, starts_lanes)[:, :2].reshape(-1)
    n_valid = (ends[-1] // TM_EXPERT).astype(jnp.int32).reshape(1)
    tile_start = jnp.arange(n_tiles, dtype=jnp.int32) * TM_EXPERT
    tile_expert = jnp.minimum(
        jnp.sum((tile_start[:, None] >= ends[None, :]).astype(jnp.int32), axis=1), MOE_EXPERTS - 1)
    return pos, ends, tile_expert, n_valid


def kernel(x, c, ada_w, ada_b, norm1_g, w_in, gla_w_gk2, gla_b_gk, gla_norm_g, conv_w, conv_b,
           dt_bias, a_log, d_skip, ssm_norm_g, w_oa, w_ob, w_out, norm2_g, router_group_w,
           router_group_b, router_expert_w, router_expert_b, expert_w1, expert_w3, expert_w2,
           final_norm_g):
    bsz, seq, d = x.shape
    depth = ada_w.shape[0]
    t = bsz * seq
    n_tiles = (2 * t) // TM_EXPERT + MOE_EXPERTS
    n_rows = n_tiles * TM_EXPERT

    mod = _modulation(c, ada_w, ada_b)
    xf = x.reshape(t, d)
    w_main, w_small = _prep_w_in(w_in)
    for l in range(depth):
        sh1, sc1, gt1, sh2, sc2, gt2 = [mod[l, :, k * d:(k + 1) * d].reshape(bsz, 1, d) for k in range(N_MOD)]
        dt_bias_pad = jnp.concatenate([jnp.zeros((GLA_RANK,), F32), dt_bias[l],
                                       jnp.zeros((LANES - GLA_RANK - SSM_HEADS,), F32)]).reshape(1, LANES)
        proj, small = _norm_proj(xf, norm1_g[l].reshape(1, d), sc1, sh1, w_main, w_small, dt_bias_pad, seq, l)

        wg_pad = jnp.concatenate([gla_w_gk2[l], jnp.zeros((LANES - GLA_RANK, GLA_DK), F32)], axis=0)
        o_a = _gla(proj, small, wg_pad, gla_b_gk[l].reshape(1, GLA_DK),
                   gla_norm_g[l].reshape(1, GLA_HV), bsz, seq)
        o_b = _ssd(proj, small, conv_w[l], conv_b[l].reshape(1, -1),
                   jnp.repeat(a_log[l], SSM_HEADDIM).reshape(1, SSM_DINNER),
                   jnp.repeat(d_skip[l], SSM_HEADDIM).reshape(1, SSM_DINNER),
                   ssm_norm_g[l].reshape(1, SSM_DINNER), bsz, seq)

        w_r = jnp.concatenate([router_group_w[l], router_expert_w[l],
                               jnp.zeros((d, LANES - MOE_GROUPS - MOE_EXPERTS), F32)], axis=1)
        b_r = jnp.concatenate([router_group_b[l], router_expert_b[l],
                               jnp.zeros((LANES - MOE_GROUPS - MOE_EXPERTS,), F32)]).reshape(1, LANES)
        x1, h2, ri, rw, cnt = _merge(o_a, o_b, proj, xf, gt1, w_oa[l].astype(BF16), w_ob[l].astype(BF16),
                                     w_out[l].astype(BF16), norm2_g[l].reshape(1, d), sc2, sh2, w_r, b_r, seq)

        pos, ends, tile_expert, n_valid = _routing_tables(ri, cnt, n_tiles)
        xs = _dispatch(pos, ends, h2, n_rows)
        ys = _experts(tile_expert, n_valid, xs,
                      expert_w1.reshape(depth * MOE_EXPERTS, d, MOE_DFF),
                      expert_w3.reshape(depth * MOE_EXPERTS, d, MOE_DFF),
                      expert_w2.reshape(depth * MOE_EXPERTS, MOE_DFF, d), l)
        xf = _combine(pos, ys, x1, rw, gt2, final_norm_g.reshape(1, d), seq, final=(l == depth - 1))
    return xf.reshape(bsz, seq, d)
```

```python
import functools

import jax
import jax.numpy as jnp
from jax import lax
from jax.experimental import pallas as pl
from jax.experimental.pallas import tpu as pltpu

F32 = jnp.float32
BF16 = jnp.bfloat16

EPS = 1e-6
CHUNK = 64
D_MODEL = 1024
N_MOD = 6
GLA_HEADS = 4
GLA_HK = 128
GLA_HV = 256
GLA_DK = GLA_HEADS * GLA_HK
GLA_DV = GLA_HEADS * GLA_HV
GLA_RANK = 16
GLA_NORMALIZER = 16.0
SSM_DINNER = 2048
SSM_HEADDIM = 64
SSM_HEADS = SSM_DINNER // SSM_HEADDIM
SSM_GROUPS = 8
SSM_HPG = SSM_HEADS // SSM_GROUPS
SSM_GW = SSM_HPG * SSM_HEADDIM
SSM_DSTATE = 128
SSM_CONV = 4
MOE_GROUPS = 4
MOE_EPG = 8
MOE_EXPERTS = MOE_GROUPS * MOE_EPG
MOE_DFF = 256

OFF_Q = 0
OFF_K = OFF_Q + GLA_DK
OFF_V = OFF_K + GLA_DK
OFF_G = OFF_V + GLA_DV
OFF_Z = OFF_G + GLA_DV
OFF_XS = OFF_Z + SSM_DINNER
OFF_B = OFF_XS + SSM_DINNER
OFF_C = OFF_B + SSM_GROUPS * SSM_DSTATE
OFF_GA = OFF_C + SSM_GROUPS * SSM_DSTATE
OFF_GB = OFF_GA + D_MODEL
N_PROJ = OFF_GB + D_MODEL
LANES = 128
SUBLANES = 8
SM_DT = GLA_RANK
RL_E = MOE_GROUPS

V7X_VMEM_LIMIT = 56 * 1024 * 1024

TM_PROJ = 1024
NJ_PROJ = 4
TN_PROJ = N_PROJ // NJ_PROJ
TM_MERGE = 512
TM_EXPERT = 512
TM_COMBINE = 256
TM_POS = 2048
DISPATCH_BLK = 1024
MIX_BLK = 256
CPB = MIX_BLK // CHUNK
CONV_ROWS = 128
CONV_HALO = 16
CONV_WIN = CONV_HALO + CONV_ROWS


def _cparams(sem):
    return pltpu.CompilerParams(dimension_semantics=sem, vmem_limit_bytes=V7X_VMEM_LIMIT)


def _sig(x):
    return 0.5 * jnp.tanh(0.5 * x) + 0.5


def _softplus(x):
    return jnp.maximum(x, 0.0) + jnp.log(1.0 + jnp.exp(-jnp.abs(x)))


def _log_sigmoid(x):
    return jnp.minimum(x, 0.0) - jnp.log(1.0 + jnp.exp(-jnp.abs(x)))


def _split(a):
    hi = a.astype(BF16)
    lo = (a - hi.astype(F32)).astype(BF16)
    return hi, lo


def _mm(a, b):
    return jnp.dot(a, b, preferred_element_type=F32)


def _dot3(a, b):
    ah, al = _split(a)
    bh, bl = _split(b)
    return _mm(ah, bh) + _mm(al, bh) + _mm(ah, bl)


def _dot2_lhs(a, b_bf16):
    ah, al = _split(a)
    return _mm(ah, b_bf16) + _mm(al, b_bf16)


def _chunk_scan_matrix(n):
    r = lax.broadcasted_iota(jnp.int32, (n, n), 0)
    c = lax.broadcasted_iota(jnp.int32, (n, n), 1)
    return jnp.where((r // CHUNK == c // CHUNK) & (r >= c), 1.0, 0.0).astype(BF16)


def _chunk_scan(mat, x):
    n, w = x.shape
    hi, lo = _split(x)
    out = _mm(mat, jnp.concatenate([hi, lo], axis=1))
    cum = out[:, :w] + out[:, w:]
    last_rows = [cum[c * CHUNK + CHUNK - 1:(c + 1) * CHUNK, :] for c in range(n // CHUNK)]
    tot = jnp.concatenate([jnp.broadcast_to(row, (CHUNK, w)) for row in last_rows], axis=0)
    return cum, tot


def _rms(x):
    return x * lax.rsqrt(jnp.mean(x * x, axis=-1, keepdims=True) + EPS)


def _pack_halves(x):
    w = x.shape[1] // 2
    lo = lax.bitcast_convert_type(x[:, :w].astype(BF16).astype(F32), jnp.uint32)
    hi = lax.bitcast_convert_type(x[:, w:].astype(BF16).astype(F32), jnp.uint32)
    return (lo >> 16) | (hi & jnp.uint32(0xFFFF0000))


def _unpack_halves(p):
    lo = lax.bitcast_convert_type(p << 16, F32)
    hi = lax.bitcast_convert_type(p & jnp.uint32(0xFFFF0000), F32)
    return lo, hi


def _skewed_loop(n, stages):
    depth = len(stages)

    def step(i, lo, hi):
        for k in reversed(range(lo, hi)):
            stages[k](i - k)

    for i in range(depth - 1):
        step(i, 0, i + 1)

    def body(i, carry):
        step(i, 0, depth)
        return carry

    lax.fori_loop(depth - 1, n, body, 0)
    for i in range(n, n + depth - 1):
        step(i, i - n + 1, depth)


def _item_rows(i):
    start = i * MIX_BLK
    return start if isinstance(start, int) else pl.multiple_of(start, MIX_BLK)


def _mod_kernel(c_ref, w_ref, b_ref, o_ref):
    c = c_ref[...]
    o_ref[0] = _dot3(c * _sig(c), w_ref[0]) + b_ref[0]


def _modulation(c, ada_w, ada_b):
    depth, d, n = ada_w.shape
    bsz = c.shape[0]
    tn = 1536
    return pl.pallas_call(
        _mod_kernel,
        grid=(depth, n // tn),
        in_specs=[
            pl.BlockSpec((bsz, d), lambda l, j: (0, 0)),
            pl.BlockSpec((1, d, tn), lambda l, j: (l, 0, j)),
            pl.BlockSpec((1, 1, tn), lambda l, j: (l, 0, j)),
        ],
        out_specs=pl.BlockSpec((1, bsz, tn), lambda l, j: (l, 0, j)),
        out_shape=jax.ShapeDtypeStruct((depth, bsz, n), F32),
        compiler_params=_cparams(("arbitrary", "arbitrary")),
        name="adaln_mod",
    )(c, ada_w, ada_b.reshape(depth, 1, n))


W_UNIT = 256
W_GK_UNIT = (OFF_G + GLA_DV) // W_UNIT
W_DT_UNIT = OFF_GA // W_UNIT


def _w_in_kernel(a_ref, b_ref, wm_ref, ws_ref, win_s):
    u = pl.program_id(1)
    win_s[:W_UNIT, :] = a_ref[0]
    win_s[W_UNIT:, :] = b_ref[0]
    cut = jnp.where(u < W_GK_UNIT, 0, jnp.where(u < W_DT_UNIT, GLA_RANK, GLA_RANK + SSM_HEADS))
    wm_ref[0] = win_s[pl.ds(pl.multiple_of(cut, GLA_RANK), W_UNIT), :].astype(BF16).T

    def side_columns():
        head = win_s[:LANES, :].T
        return head, lax.broadcasted_iota(jnp.int32, head.shape, 1)

    @pl.when(u == W_GK_UNIT)
    def _():
        head, lane = side_columns()
        ws_ref[0] = jnp.where(lane < GLA_RANK, head, 0.0)

    @pl.when(u == W_DT_UNIT)
    def _():
        head, lane = side_columns()
        ws_ref[0] = jnp.where((lane >= SM_DT) & (lane < SM_DT + SSM_HEADS), head, ws_ref[0])


def _prep_w_in(w_in_t):
    depth, _, d = w_in_t.shape
    return pl.pallas_call(
        _w_in_kernel,
        grid=(depth, N_PROJ // W_UNIT),
        in_specs=[
            pl.BlockSpec((1, W_UNIT, d), lambda l, u: (l, u, 0)),
            pl.BlockSpec((1, LANES, d), lambda l, u: (l, (W_UNIT // LANES) * (u + 1), 0)),
        ],
        out_specs=[
            pl.BlockSpec((1, d, W_UNIT), lambda l, u: (l, 0, u)),
            pl.BlockSpec((1, d, LANES), lambda l, u: (l, 0, 0)),
        ],
        out_shape=[
            jax.ShapeDtypeStruct((depth, d, N_PROJ), BF16),
            jax.ShapeDtypeStruct((depth, d, LANES), F32),
        ],
        scratch_shapes=[pltpu.VMEM((W_UNIT + LANES, d), F32)],
        compiler_params=_cparams(("arbitrary", "arbitrary")),
        name="w_in_layout",
    )(w_in_t, w_in_t)


def _norm_proj_kernel(x0_ref, xn_ref, g_ref, sc0_ref, sh0_ref, scn_ref, shn_ref, w_ref, ws_ref, dtb_ref,
                      o_ref, os_ref, h_s, side_s):
    i = pl.program_id(0)
    j = pl.program_id(1)
    cur = i % 2
    rows_per_step = TM_PROJ // NJ_PROJ

    def norm_rows(x_ref, sc_ref, sh_ref, rows, slot):
        h = _rms(x_ref[rows, :]) * g_ref[...] * (1.0 + sc_ref[0]) + sh_ref[0]
        h_s[slot, rows, :] = h.astype(BF16)
        side = _dot3(h, ws_ref[...])
        lane = lax.broadcasted_iota(jnp.int32, side.shape, 1)
        is_dt = (lane >= SM_DT) & (lane < SM_DT + SSM_HEADS)
        side_s[slot, rows, :] = jnp.where(is_dt, _softplus(side + dtb_ref[...]), side)

    @pl.when((i == 0) & (j == 0))
    def _():
        for r in range(NJ_PROJ):
            norm_rows(x0_ref, sc0_ref, sh0_ref, pl.ds(r * rows_per_step, rows_per_step), 0)

    norm_rows(xn_ref, scn_ref, shn_ref,
              pl.ds(pl.multiple_of(j * rows_per_step, rows_per_step), rows_per_step), 1 - cur)

    @pl.when(j == 0)
    def _():
        os_ref[...] = side_s[cur]

    o_ref[...] = _mm(h_s[cur], w_ref[...]).astype(BF16)


def _norm_proj(x, g, sc, sh, w_main, w_small, dt_bias_pad, seq, layer):
    t, d = x.shape
    tpb = seq // TM_PROJ
    last = t // TM_PROJ - 1
    nxt = lambda i: jnp.minimum(i + 1, last)
    return pl.pallas_call(
        _norm_proj_kernel,
        grid=(t // TM_PROJ, NJ_PROJ),
        in_specs=[
            pl.BlockSpec((TM_PROJ, d), lambda i, j: (0, 0)),
            pl.BlockSpec((TM_PROJ, d), lambda i, j: (nxt(i), 0)),
            pl.BlockSpec((1, d), lambda i, j: (0, 0)),
            pl.BlockSpec((1, 1, d), lambda i, j: (0, 0, 0)),
            pl.BlockSpec((1, 1, d), lambda i, j: (0, 0, 0)),
            pl.BlockSpec((1, 1, d), lambda i, j: (nxt(i) // tpb, 0, 0)),
            pl.BlockSpec((1, 1, d), lambda i, j: (nxt(i) // tpb, 0, 0)),
            pl.BlockSpec((None, d, TN_PROJ), lambda i, j: (layer, 0, j)),
            pl.BlockSpec((None, d, LANES), lambda i, j: (layer, 0, 0)),
            pl.BlockSpec((1, LANES), lambda i, j: (0, 0)),
        ],
        out_specs=[
            pl.BlockSpec((TM_PROJ, TN_PROJ), lambda i, j: (i, j)),
            pl.BlockSpec((TM_PROJ, LANES), lambda i, j: (i, 0)),
        ],
        out_shape=[
            jax.ShapeDtypeStruct((t, N_PROJ), BF16),
            jax.ShapeDtypeStruct((t, LANES), F32),
        ],
        scratch_shapes=[pltpu.VMEM((2, TM_PROJ, d), BF16), pltpu.VMEM((2, TM_PROJ, LANES), F32)],
        compiler_params=_cparams(("arbitrary", "arbitrary")),
        name="norm_in_proj",
    )(x, x, g, sc, sh, sc, sh, w_main, w_small, dt_bias_pad)


def _gla_kernel(q_ref, k_ref, v_ref, g_ref, sm_ref, wg_ref, bg_ref, ng_ref, o_ref,
                gk_s, kdec_s, edec_s, upd_s, st_s, oraw_s):
    seq = q_ref.shape[0]
    scan = _chunk_scan_matrix(MIX_BLK)
    wg_hi, wg_lo = _split(wg_ref[...])
    bg = bg_ref[...]
    ng = ng_ref[...]
    scale = GLA_HK ** -0.5
    st_s[...] = jnp.zeros_like(st_s)

    def stage_gate(i):
        rows = pl.ds(_item_rows(i), MIX_BLK)
        sm_hi, sm_lo = _split(sm_ref[rows, :])
        pre = _mm(sm_hi, wg_hi) + _mm(sm_lo, wg_hi) + _mm(sm_hi, wg_lo) + bg
        gk_s[rows, :] = _log_sigmoid(pre) * (1.0 / GLA_NORMALIZER)

    def stage_decay(i):
        rows = pl.ds(_item_rows(i), MIX_BLK)
        cum, tot = _chunk_scan(scan, gk_s[rows, :])
        kdec_s[rows, :] = (k_ref[rows, :].astype(F32) * jnp.exp(tot - cum)).astype(BF16)
        edec_s[rows, :] = jnp.exp(tot)

    def stage_update(i):
        for c in range(CPB):
            rows = pl.ds(_item_rows(i) + c * CHUNK, CHUNK)
            upd_s[i * CPB + c] = lax.dot_general(
                v_ref[rows, :], kdec_s[rows, :], (((0,), (0,)), ((), ())), preferred_element_type=F32)

    def stage_state(i):
        st = st_s[...]
        for c in range(CPB):
            r0 = _item_rows(i) + c * CHUNK
            st = edec_s[pl.ds(r0, 1), :] * st + upd_s[i * CPB + c]
            oraw_s[pl.ds(r0, CHUNK), :] = lax.dot_general(
                q_ref[pl.ds(r0, CHUNK), :], st.astype(BF16), (((1,), (1,)), ((), ())),
                preferred_element_type=F32)
        st_s[...] = st

    def stage_out(i):
        rows = pl.ds(_item_rows(i), MIX_BLK)
        o = _rms(oraw_s[rows, :] * scale) * ng
        g = g_ref[rows, :].astype(F32)
        o_ref[rows, :] = (o * (g * _sig(g))).astype(BF16)

    _skewed_loop(seq // MIX_BLK, [stage_gate, stage_decay, stage_update, stage_state, stage_out])


def _gla(proj, small, wg_pad, b_gk, norm_g, bsz, seq):
    t = proj.shape[0]
    kq, kk = OFF_Q // GLA_HK, OFF_K // GLA_HK
    kv, kg = OFF_V // GLA_HV, OFF_G // GLA_HV
    return pl.pallas_call(
        _gla_kernel,
        grid=(bsz, GLA_HEADS),
        in_specs=[
            pl.BlockSpec((seq, GLA_HK), lambda b, h: (b, kq + h)),
            pl.BlockSpec((seq, GLA_HK), lambda b, h: (b, kk + h)),
            pl.BlockSpec((seq, GLA_HV), lambda b, h: (b, kv + h)),
            pl.BlockSpec((seq, GLA_HV), lambda b, h: (b, kg + h)),
            pl.BlockSpec((seq, LANES), lambda b, h: (b, 0)),
            pl.BlockSpec((LANES, GLA_HK), lambda b, h: (0, h)),
            pl.BlockSpec((1, GLA_HK), lambda b, h: (0, h)),
            pl.BlockSpec((1, GLA_HV), lambda b, h: (0, 0)),
        ],
        out_specs=pl.BlockSpec((seq, GLA_HV), lambda b, h: (b, h)),
        out_shape=jax.ShapeDtypeStruct((t, GLA_DV), BF16),
        scratch_shapes=[
            pltpu.VMEM((seq, GLA_HK), F32),
            pltpu.VMEM((seq, GLA_HK), BF16),
            pltpu.VMEM((seq, GLA_HK), F32),
            pltpu.VMEM((seq // CHUNK, GLA_HV, GLA_HK), F32),
            pltpu.VMEM((GLA_HV, GLA_HK), F32),
            pltpu.VMEM((seq, GLA_HV), F32),
        ],
        compiler_params=_cparams(("arbitrary", "arbitrary")),
        name="gla_mixer",
    )(proj, proj, proj, proj, small, wg_pad, b_gk, norm_g)


def _conv_shift_matrix():
    t = lax.broadcasted_iota(jnp.int32, (CONV_ROWS, SSM_CONV * CONV_WIN), 0)
    c = lax.broadcasted_iota(jnp.int32, (CONV_ROWS, SSM_CONV * CONV_WIN), 1)
    return jnp.where(c % CONV_WIN == t + CONV_HALO - c // CONV_WIN, 1.0, 0.0).astype(BF16)


def _conv_window(ref, r0):
    if isinstance(r0, int) and r0 == 0:
        return jnp.concatenate([jnp.zeros((CONV_HALO, ref.shape[1]), BF16), ref[0:CONV_ROWS, :]], axis=0)
    start = r0 - CONV_HALO
    return ref[pl.ds(start if isinstance(start, int) else pl.multiple_of(start, CONV_HALO), CONV_WIN), :]


def _ssd_kernel(z_ref, xs_ref, b_ref, c_ref, sm_ref, cwx_ref, cwb_ref, cwc_ref, cbx_ref, cbb_ref,
                cbc_ref, alog_ref, dsk_ref, ng_ref, o_ref,
                xs_s, b_s, c_s, dt_s, a_s, ea_s, edec_s, xdt_s, decx_s, m_s, upd_s, st_s, y_s):
    seq = z_ref.shape[0]
    grp = pl.program_id(1)
    er = lax.broadcasted_iota(jnp.int32, (LANES, SSM_GW), 0)
    el = lax.broadcasted_iota(jnp.int32, (LANES, SSM_GW), 1)
    expand = jnp.where(er == SM_DT + SSM_HPG * grp + el // SSM_HEADDIM, 1.0, 0.0).astype(BF16)
    scan = _chunk_scan_matrix(MIX_BLK)
    a_row = -jnp.exp(alog_ref[...])
    dsk = dsk_ref[...]
    ng = ng_ref[...]
    dr = lax.broadcasted_iota(jnp.int32, (CHUNK, SSM_GW), 0)
    dl = lax.broadcasted_iota(jnp.int32, (CHUNK, SSM_GW), 1)
    diag = jnp.where(dr == dl % CHUNK, 1.0, 0.0)
    br = lax.broadcasted_iota(jnp.int32, (SSM_GW, SSM_GW), 0)
    bl = lax.broadcasted_iota(jnp.int32, (SSM_GW, SSM_GW), 1)
    blockmask = jnp.where(br // SSM_HEADDIM == bl // SSM_HEADDIM, 1.0, 0.0).astype(BF16)
    st_s[...] = jnp.zeros_like(st_s)
    shift = _conv_shift_matrix()
    taps = jnp.concatenate([cwx_ref[...], cwb_ref[...], cwc_ref[...]], axis=1)
    conv_bias = jnp.concatenate([cbx_ref[...], cbb_ref[...], cbc_ref[...]], axis=1)

    def stage_conv(i):
        rows = pl.ds(_item_rows(i), MIX_BLK)
        dt_s[rows, :] = _dot2_lhs(sm_ref[rows, :], expand)
        for sub in range(MIX_BLK // CONV_ROWS):
            r0 = _item_rows(i) + sub * CONV_ROWS
            win = jnp.concatenate(
                [_conv_window(xs_ref, r0), _conv_window(b_ref, r0), _conv_window(c_ref, r0)], axis=1)
            scaled = jnp.concatenate(
                [win * taps[SSM_CONV - 1 - k:SSM_CONV - k, :].astype(BF16) for k in range(SSM_CONV)], axis=0)
            acc = _mm(shift, scaled) + conv_bias
            act = acc * _sig(acc)
            out_rows = pl.ds(r0, CONV_ROWS)
            xs_s[out_rows, :] = act[:, :SSM_GW]
            b_s[out_rows, :] = act[:, SSM_GW:SSM_GW + SSM_DSTATE].astype(BF16)
            c_s[out_rows, :] = act[:, SSM_GW + SSM_DSTATE:].astype(BF16)

    def stage_decay(i):
        rows = pl.ds(_item_rows(i), MIX_BLK)
        dt = dt_s[rows, :]
        acum, atot = _chunk_scan(scan, dt * a_row)
        a_s[rows, :] = acum
        ea_s[rows, :] = jnp.exp(acum)
        edec_s[rows, :] = jnp.exp(atot)
        xdt = xs_s[rows, :] * dt
        xdt_s[rows, :] = xdt.astype(BF16)
        decx_s[rows, :] = (xdt * jnp.exp(atot - acum)).astype(BF16)

    def stage_intra(i):
        for c in range(CPB):
            rows = pl.ds(_item_rows(i) + c * CHUNK, CHUNK)
            acum = a_s[rows, :]
            acum_s = jnp.sum(acum * diag, axis=0, keepdims=True)
            lmask = jnp.exp(-jnp.abs(acum - acum_s))
            bm = b_s[rows, :]
            cb = lax.dot_general(c_s[rows, :], jnp.concatenate([bm] * SSM_HPG, axis=0),
                                 (((1,), (1,)), ((), ())), preferred_element_type=F32)
            m_s[rows, :] = (cb * lmask).astype(BF16)
            upd_s[i * CPB + c] = lax.dot_general(bm, decx_s[rows, :], (((0,), (0,)), ((), ())),
                                                 preferred_element_type=F32)

    def stage_state(i):
        st = st_s[...]
        for c in range(CPB):
            r0 = _item_rows(i) + c * CHUNK
            rows = pl.ds(r0, CHUNK)
            xdt_bd = jnp.concatenate([xdt_s[rows, :]] * SSM_HPG, axis=0) * blockmask
            y = _mm(m_s[rows, :], xdt_bd) + _mm(c_s[rows, :], st.astype(BF16)) * ea_s[rows, :]
            st = edec_s[pl.ds(r0, 1), :] * st + upd_s[i * CPB + c]
            y = y + xs_s[rows, :] * dsk
            z = z_ref[rows, :].astype(F32)
            y_s[rows, :] = y * (z * _sig(z))
        st_s[...] = st

    def stage_out(i):
        rows = pl.ds(_item_rows(i), MIX_BLK)
        o_ref[rows, :] = (_rms(y_s[rows, :]) * ng).astype(BF16)

    _skewed_loop(seq // MIX_BLK, [stage_conv, stage_decay, stage_intra, stage_state, stage_out])


def _ssd(proj, small, conv_w, conv_b, a_log_e, d_skip_e, norm_g, bsz, seq):
    t = proj.shape[0]
    kz, kx = OFF_Z // SSM_GW, OFF_XS // SSM_GW
    kb, kc = OFF_B // SSM_DSTATE, OFF_C // SSM_DSTATE
    nb = SSM_DINNER // SSM_DSTATE
    ncg = SSM_GROUPS
    wide_f32 = pltpu.VMEM((seq, SSM_GW), F32)
    wide_bf16 = pltpu.VMEM((seq, SSM_GW), BF16)
    return pl.pallas_call(
        _ssd_kernel,
        grid=(bsz, SSM_GROUPS),
        in_specs=[
            pl.BlockSpec((seq, SSM_GW), lambda b, g: (b, kz + g)),
            pl.BlockSpec((seq, SSM_GW), lambda b, g: (b, kx + g)),
            pl.BlockSpec((seq, SSM_DSTATE), lambda b, g: (b, kb + g)),
            pl.BlockSpec((seq, SSM_DSTATE), lambda b, g: (b, kc + g)),
            pl.BlockSpec((seq, LANES), lambda b, g: (b, 0)),
            pl.BlockSpec((SSM_CONV, SSM_GW), lambda b, g: (0, g)),
            pl.BlockSpec((SSM_CONV, SSM_DSTATE), lambda b, g: (0, nb + g)),
            pl.BlockSpec((SSM_CONV, SSM_DSTATE), lambda b, g: (0, nb + ncg + g)),
            pl.BlockSpec((1, SSM_GW), lambda b, g: (0, g)),
            pl.BlockSpec((1, SSM_DSTATE), lambda b, g: (0, nb + g)),
            pl.BlockSpec((1, SSM_DSTATE), lambda b, g: (0, nb + ncg + g)),
            pl.BlockSpec((1, SSM_GW), lambda b, g: (0, g)),
            pl.BlockSpec((1, SSM_GW), lambda b, g: (0, g)),
            pl.BlockSpec((1, SSM_GW), lambda b, g: (0, g)),
        ],
        out_specs=pl.BlockSpec((seq, SSM_GW), lambda b, g: (b, g)),
        out_shape=jax.ShapeDtypeStruct((t, SSM_DINNER), BF16),
        scratch_shapes=[
            wide_f32,
            pltpu.VMEM((seq, SSM_DSTATE), BF16),
            pltpu.VMEM((seq, SSM_DSTATE), BF16),
            wide_f32,
            wide_f32,
            wide_f32,
            wide_f32,
            wide_bf16,
            wide_bf16,
            wide_bf16,
            pltpu.VMEM((seq // CHUNK, SSM_DSTATE, SSM_GW), F32),
            pltpu.VMEM((SSM_DSTATE, SSM_GW), F32),
            wide_f32,
        ],
        compiler_params=_cparams(("arbitrary", "arbitrary")),
        name="ssd_mixer",
    )(proj, proj, proj, proj, small, conv_w, conv_w, conv_w, conv_b, conv_b, conv_b,
      a_log_e, d_skip_e, norm_g)


def _lane_min(x):
    return jnp.min(x, axis=-1, keepdims=True)


def _lane_max(x):
    return jnp.max(x, axis=-1, keepdims=True)


def _merge_kernel(oa_ref, ob_ref, ga_ref, gb_ref, x_ref, gt_ref, woa_ref, wob_ref, wout_ref,
                  g2_ref, sc_ref, sh_ref, wr_ref, br_ref,
                  x1_ref, h2_ref, ri_ref, rw_ref, cnt_ref, cnt_s, h_s):
    i = pl.program_id(0)
    tm = x_ref.shape[0]

    @pl.when(i == 0)
    def _():
        cnt_s[...] = jnp.zeros_like(cnt_s)
        h_s[...] = jnp.zeros_like(h_s)

    logit = _dot3(h_s[...], wr_ref[...]) + br_ref[...]

    ya = _mm(oa_ref[...], woa_ref[...])
    yb = _mm(ob_ref[...], wob_ref[...])

    lane = lax.broadcasted_iota(jnp.int32, (tm, LANES), 1)
    lane_f = lane.astype(F32)
    neg = jnp.float32(-jnp.inf)
    big = jnp.float32(LANES)
    is_g = lane < MOE_GROUPS
    lg = jnp.where(is_g, logit, neg)
    gmax = _lane_max(lg)
    gsum = jnp.sum(jnp.where(is_g, jnp.exp(lg - gmax), 0.0), axis=-1, keepdims=True)
    gidx = _lane_min(jnp.where(lg == gmax, lane_f, big))
    lo = RL_E + MOE_EPG * gidx
    in_grp = (lane_f >= lo) & (lane_f < lo + MOE_EPG)
    le = jnp.where(in_grp, logit, neg)
    emax = _lane_max(le)
    l0 = _lane_min(jnp.where(le == emax, lane_f, big))
    le2 = jnp.where(lane_f == l0, neg, le)
    emax2 = _lane_max(le2)
    l1 = _lane_min(jnp.where(le2 == emax2, lane_f, big))
    ratio = jnp.exp(emax2 - emax)
    w0 = 1.0 / (gsum * (1.0 + ratio))
    w1 = w0 * ratio

    counted = jnp.where(i > 0, 1.0, 0.0)
    oh0 = lane_f == l0
    oh1 = lane_f == l1
    oh = jnp.where(oh0 | oh1, counted, 0.0)
    tr = lax.broadcasted_iota(jnp.int32, (tm, tm), 0)
    tc = lax.broadcasted_iota(jnp.int32, (tm, tm), 1)
    before = _mm(jnp.where(tr > tc, 1.0, 0.0).astype(BF16), oh.astype(BF16)) + cnt_s[0:1, :]
    r0 = jnp.sum(jnp.where(oh0, before, 0.0), axis=-1, keepdims=True)
    r1 = jnp.sum(jnp.where(oh1, before, 0.0), axis=-1, keepdims=True)
    total = cnt_s[0:1, :] + jnp.sum(oh, axis=0, keepdims=True)
    cnt_s[...] = jnp.broadcast_to(total, cnt_s.shape)
    cnt_ref[...] = jnp.broadcast_to(total, cnt_ref.shape)

    ri = jnp.where(lane == 0, l0 - RL_E,
                   jnp.where(lane == 1, l1 - RL_E, jnp.where(lane == 2, r0, jnp.where(lane == 3, r1, 0.0))))
    ri_ref[...] = ri.astype(jnp.int32)
    rw_ref[...] = jnp.where(lane == 0, w0, jnp.where(lane == 1, w1, 0.0))

    merged = _sig(ga_ref[...].astype(F32)) * ya + _sig(gb_ref[...].astype(F32)) * yb
    x1 = x_ref[...] + gt_ref[0] * _mm(merged.astype(BF16), wout_ref[...])
    x1_ref[...] = x1
    h = _rms(x1) * g2_ref[...] * (1.0 + sc_ref[0]) + sh_ref[0]
    h2_ref[...] = _pack_halves(h)
    h_s[...] = h


def _merge(o_a, o_b, proj, x, gt1, w_oa, w_ob, w_out, g2, sc2, sh2, w_r, b_r, seq):
    t, d = x.shape
    tm = TM_MERGE
    tpb = seq // tm
    n = t // tm
    kga, kgb = OFF_GA // d, OFF_GB // d
    const = lambda i: (0, 0)
    cur = lambda i: jnp.minimum(i, n - 1)
    prev = lambda i: jnp.maximum(i - 1, 0)
    return pl.pallas_call(
        _merge_kernel,
        grid=(n + 1,),
        in_specs=[
            pl.BlockSpec((tm, GLA_DV), lambda i: (cur(i), 0)),
            pl.BlockSpec((tm, SSM_DINNER), lambda i: (cur(i), 0)),
            pl.BlockSpec((tm, d), lambda i: (cur(i), kga)),
            pl.BlockSpec((tm, d), lambda i: (cur(i), kgb)),
            pl.BlockSpec((tm, d), lambda i: (cur(i), 0)),
            pl.BlockSpec((1, 1, d), lambda i: (cur(i) // tpb, 0, 0)),
            pl.BlockSpec((GLA_DV, d), const),
            pl.BlockSpec((SSM_DINNER, d), const),
            pl.BlockSpec((d, d), const),
            pl.BlockSpec((1, d), const),
            pl.BlockSpec((1, 1, d), lambda i: (cur(i) // tpb, 0, 0)),
            pl.BlockSpec((1, 1, d), lambda i: (cur(i) // tpb, 0, 0)),
            pl.BlockSpec((d, LANES), const),
            pl.BlockSpec((1, LANES), const),
        ],
        out_specs=[
            pl.BlockSpec((tm, d), lambda i: (cur(i), 0)),
            pl.BlockSpec((tm, d // 2), lambda i: (cur(i), 0)),
            pl.BlockSpec((tm, LANES), lambda i: (prev(i), 0)),
            pl.BlockSpec((tm, LANES), lambda i: (prev(i), 0)),
            pl.BlockSpec((8, LANES), const),
        ],
        out_shape=[
            jax.ShapeDtypeStruct((t, d), F32),
            jax.ShapeDtypeStruct((t, d // 2), jnp.uint32),
            jax.ShapeDtypeStruct((t, LANES), jnp.int32),
            jax.ShapeDtypeStruct((t, LANES), F32),
            jax.ShapeDtypeStruct((8, LANES), F32),
        ],
        scratch_shapes=[pltpu.VMEM((8, LANES), F32), pltpu.VMEM((tm, d), F32)],
        compiler_params=_cparams(("arbitrary",)),
        name="merge_router",
    )(o_a, o_b, proj, proj, x, gt1, w_oa, w_ob, w_out, g2, sc2, sh2, w_r, b_r)


def _pos_kernel(ri_ref, st_ref, pos_ref):
    ri = ri_ref[...].astype(F32)
    lane = lax.broadcasted_iota(jnp.int32, ri.shape, 1)
    lane_f = lane.astype(F32)
    starts = st_ref[...]
    p0 = jnp.sum(jnp.where(lane_f == ri[:, 0:1], starts, 0.0), axis=-1, keepdims=True) + ri[:, 2:3]
    p1 = jnp.sum(jnp.where(lane_f == ri[:, 1:2], starts, 0.0), axis=-1, keepdims=True) + ri[:, 3:4]
    pos_ref[...] = jnp.where(lane == 0, p0, jnp.where(lane == 1, p1, 0.0)).astype(jnp.int32)


def _positions(ri, starts_lanes):
    t = ri.shape[0]
    return pl.pallas_call(
        _pos_kernel,
        grid=(t // TM_POS,),
        in_specs=[
            pl.BlockSpec((TM_POS, LANES), lambda i: (i, 0)),
            pl.BlockSpec((1, LANES), lambda i: (0, 0)),
        ],
        out_specs=pl.BlockSpec((TM_POS, LANES), lambda i: (i, 0)),
        out_shape=jax.ShapeDtypeStruct((t, LANES), jnp.int32),
        compiler_params=_cparams(("arbitrary",)),
        name="moe_positions",
    )(ri, starts_lanes)


def _dispatch_kernel(pos_ref, ends_ref, h_ref, xs_ref, zero_s, sem, zsem):
    n_tok = h_ref.shape[0] * SUBLANES

    @pl.when(pl.program_id(0) == 0)
    def _():
        zero_s[...] = jnp.zeros_like(zero_s)

        def last_tile(e):
            end = ends_ref[e]
            begin = jnp.where(e > 0, ends_ref[jnp.maximum(e - 1, 0)], 0)
            dst = xs_ref.at[pl.ds(pl.multiple_of(jnp.maximum(end - TM_EXPERT, 0), TM_EXPERT), TM_EXPERT)]
            return end > begin, pltpu.make_async_copy(zero_s, dst, zsem)

        def fill(e, carry):
            nonempty, cp = last_tile(e)

            @pl.when(nonempty)
            def _():
                cp.start()

            return carry

        def fill_done(e, carry):
            nonempty, cp = last_tile(e)

            @pl.when(nonempty)
            def _():
                cp.wait()

            return carry

        n_used = ends_ref[MOE_EXPERTS - 1] // TM_EXPERT

        def unused_tile(tile):
            dst = xs_ref.at[pl.ds(pl.multiple_of(tile * TM_EXPERT, TM_EXPERT), TM_EXPERT)]
            return pltpu.make_async_copy(zero_s, dst, zsem)

        def tail(tile, carry):
            unused_tile(tile).start()
            return carry

        def tail_done(tile, carry):
            unused_tile(tile).wait()
            return carry

        n_tiles = xs_ref.shape[0] // TM_EXPERT
        lax.fori_loop(0, MOE_EXPERTS, fill, 0)
        lax.fori_loop(n_used, n_tiles, tail, 0)
        lax.fori_loop(0, MOE_EXPERTS, fill_done, 0)
        lax.fori_loop(n_used, n_tiles, tail_done, 0)

    def issue(g, carry):
        for r in range(SUBLANES):
            tok = g * SUBLANES + r
            for k in range(2):
                pltpu.make_async_copy(h_ref.at[g, pl.ds(r, 1)], xs_ref.at[pl.ds(pos_ref[2 * tok + k], 1)],
                                      sem).start()
        return carry

    lax.fori_loop(0, n_tok // SUBLANES, issue, 0)

    def drain(tok, carry):
        for k in range(2):
            pltpu.make_async_copy(h_ref.at[0, pl.ds(0, 1)], xs_ref.at[pl.ds(0, 1)], sem).wait()
        return carry

    lax.fori_loop(0, n_tok, drain, 0, unroll=8)


def _dispatch(pos_flat, ends, h2, n_rows):
    t, d = h2.shape
    return pl.pallas_call(
        _dispatch_kernel,
        grid=(t // DISPATCH_BLK,),
        in_specs=[
            pl.BlockSpec((2 * DISPATCH_BLK,), lambda i: (i,), memory_space=pltpu.SMEM),
            pl.BlockSpec(memory_space=pltpu.SMEM),
            pl.BlockSpec((DISPATCH_BLK // SUBLANES, SUBLANES, d), lambda i: (i, 0, 0)),
        ],
        out_specs=pl.BlockSpec(memory_space=pl.ANY),
        out_shape=jax.ShapeDtypeStruct((n_rows, d), h2.dtype),
        scratch_shapes=[pltpu.VMEM((TM_EXPERT, d), h2.dtype), pltpu.SemaphoreType.DMA(()),
                        pltpu.SemaphoreType.DMA(())],
        compiler_params=_cparams(("arbitrary",)),
        name="moe_dispatch",
    )(pos_flat, ends, h2.reshape(t // SUBLANES, SUBLANES, d))


def _expert_kernel(te_ref, nv_ref, x_ref, w1_ref, w3_ref, w2_ref, o_ref, w13_s, w2_s):
    i = pl.program_id(0)
    valid = i < nv_ref[0]
    new_expert = (i == 0) | (te_ref[i] != te_ref[jnp.maximum(i - 1, 0)])

    @pl.when(valid & new_expert)
    def _():
        w13_s[:, :MOE_DFF] = w1_ref[0].astype(BF16)
        w13_s[:, MOE_DFF:] = w3_ref[0].astype(BF16)
        w2_s[...] = w2_ref[0].astype(BF16)

    @pl.when(valid)
    def _():
        half = w13_s.shape[0] // 2
        x_lo, x_hi = _unpack_halves(x_ref[...])
        ab = _mm(x_lo.astype(BF16), w13_s[:half, :]) + _mm(x_hi.astype(BF16), w13_s[half:, :])
        a = ab[:, :MOE_DFF]
        o_ref[...] = _pack_halves(_mm((a * _sig(a) * ab[:, MOE_DFF:]).astype(BF16), w2_s[...]))

    @pl.when(jnp.logical_not(valid))
    def _():
        o_ref[...] = jnp.zeros_like(o_ref)


def _experts(tile_expert, n_valid, xs, w1, w3, w2, layer):
    n_rows, dp = xs.shape
    d = 2 * dp
    tm = TM_EXPERT
    first = layer * MOE_EXPERTS
    grid_spec = pltpu.PrefetchScalarGridSpec(
        num_scalar_prefetch=2,
        grid=(n_rows // tm,),
        in_specs=[
            pl.BlockSpec((tm, dp), lambda i, te, nv: (jnp.minimum(i, nv[0] - 1), 0)),
            pl.BlockSpec((1, d, MOE_DFF), lambda i, te, nv: (first + te[i], 0, 0)),
            pl.BlockSpec((1, d, MOE_DFF), lambda i, te, nv: (first + te[i], 0, 0)),
            pl.BlockSpec((1, MOE_DFF, d), lambda i, te, nv: (first + te[i], 0, 0)),
        ],
        out_specs=pl.BlockSpec((tm, dp), lambda i, te, nv: (i, 0)),
        scratch_shapes=[pltpu.VMEM((d, 2 * MOE_DFF), BF16), pltpu.VMEM((MOE_DFF, d), BF16)],
    )
    return pl.pallas_call(
        _expert_kernel,
        grid_spec=grid_spec,
        out_shape=jax.ShapeDtypeStruct((n_rows, dp), jnp.uint32),
        compiler_params=_cparams(("arbitrary",)),
        name="moe_experts",
    )(tile_expert, n_valid, xs, w1, w3, w2)


def _combine_kernel(pos_ref, pos_next_ref, ys_ref, x_ref, rw_ref, gt_ref, fg_ref, o_ref, buf, sem, *, final):
    tm = x_ref.shape[0]
    i = pl.program_id(0)
    slot = i % 2

    def gather(p_ref, s):
        def issue(g, carry):
            for r in range(SUBLANES):
                tok = g * SUBLANES + r
                for k in range(2):
                    pltpu.make_async_copy(ys_ref.at[pl.ds(p_ref[2 * tok + k], 1)],
                                          buf.at[s, k, g, pl.ds(r, 1)], sem.at[s]).start()
            return carry

        lax.fori_loop(0, tm // SUBLANES, issue, 0)

    @pl.when(i == 0)
    def _():
        gather(pos_ref, 0)

    @pl.when(i + 1 < pl.num_programs(0))
    def _():
        gather(pos_next_ref, 1 - slot)

    def drain(tok, carry):
        for k in range(2):
            pltpu.make_async_copy(ys_ref.at[pl.ds(0, 1)], buf.at[slot, k, 0, pl.ds(0, 1)], sem.at[slot]).wait()
        return carry

    lax.fori_loop(0, tm, drain, 0, unroll=8)

    rw = rw_ref[...]
    w0, w1 = rw[:, 0:1], rw[:, 1:2]
    y0_lo, y0_hi = _unpack_halves(buf[slot, 0].reshape(tm, -1))
    y1_lo, y1_hi = _unpack_halves(buf[slot, 1].reshape(tm, -1))
    y = jnp.concatenate([w0 * y0_lo + w1 * y1_lo, w0 * y0_hi + w1 * y1_hi], axis=1)
    out = x_ref[...] + gt_ref[0] * y
    if final:
        out = _rms(out) * fg_ref[...]
    o_ref[...] = out


def _combine(pos_flat, ys, x1, rw, gt2, final_g, seq, final):
    t, d = x1.shape
    tm = TM_COMBINE
    tpb = seq // tm
    last = t // tm - 1
    return pl.pallas_call(
        functools.partial(_combine_kernel, final=final),
        grid=(t // tm,),
        in_specs=[
            pl.BlockSpec((2 * tm,), lambda i: (i,), memory_space=pltpu.SMEM),
            pl.BlockSpec((2 * tm,), lambda i: (jnp.minimum(i + 1, last),), memory_space=pltpu.SMEM),
            pl.BlockSpec(memory_space=pl.ANY),
            pl.BlockSpec((tm, d), lambda i: (i, 0)),
            pl.BlockSpec((tm, LANES), lambda i: (i, 0)),
            pl.BlockSpec((1, 1, d), lambda i: (i // tpb, 0, 0)),
            pl.BlockSpec((1, d), lambda i: (0, 0)),
        ],
        out_specs=pl.BlockSpec((tm, d), lambda i: (i, 0)),
        out_shape=jax.ShapeDtypeStruct((t, d), F32),
        scratch_shapes=[pltpu.VMEM((2, 2, tm // SUBLANES, SUBLANES, d // 2), jnp.uint32),
                        pltpu.SemaphoreType.DMA((2,))],
        compiler_params=_cparams(("arbitrary",)),
        name="moe_combine",
    )(pos_flat, pos_flat, ys, x1, rw, gt2, final_g)


def _routing_tables(ri, cnt, n_tiles):
    counts = cnt[0, RL_E:RL_E + MOE_EXPERTS].astype(jnp.int32)
    padded = ((counts + TM_EXPERT - 1) // TM_EXPERT) * TM_EXPERT
    ends = jnp.cumsum(padded)
    starts = ends - padded
    starts_lanes = jnp.concatenate(
        [starts.astype(F32), jnp.zeros((LANES - MOE_EXPERTS,), F32)]).reshape(1, LANES)
    pos = _positions(ri, starts_lanes)[:, :2].reshape(-1)
    n_valid = (ends[-1] // TM_EXPERT).astype(jnp.int32).reshape(1)
    tile_start = jnp.arange(n_tiles, dtype=jnp.int32) * TM_EXPERT
    tile_expert = jnp.minimum(
        jnp.sum((tile_start[:, None] >= ends[None, :]).astype(jnp.int32), axis=1), MOE_EXPERTS - 1)
    return pos, ends, tile_expert, n_valid


def kernel(x, c, ada_w, ada_b, norm1_g, w_in, gla_w_gk2, gla_b_gk, gla_norm_g, conv_w, conv_b,
           dt_bias, a_log, d_skip, ssm_norm_g, w_oa, w_ob, w_out, norm2_g, router_group_w,
           router_group_b, router_expert_w, router_expert_b, expert_w1, expert_w3, expert_w2,
           final_norm_g):
    bsz, seq, d = x.shape
    depth = ada_w.shape[0]
    t = bsz * seq
    n_tiles = (2 * t) // TM_EXPERT + MOE_EXPERTS
    n_rows = n_tiles * TM_EXPERT

    mod = _modulation(c, ada_w, ada_b)
    xf = x.reshape(t, d)
    w_main, w_small = _prep_w_in(jnp.swapaxes(w_in, 1, 2))
    for l in range(depth):
        sh1, sc1, gt1, sh2, sc2, gt2 = [mod[l, :, k * d:(k + 1) * d].reshape(bsz, 1, d) for k in range(N_MOD)]
        dt_bias_pad = jnp.concatenate([jnp.zeros((GLA_RANK,), F32), dt_bias[l],
                                       jnp.zeros((LANES - GLA_RANK - SSM_HEADS,), F32)]).reshape(1, LANES)
        proj, small = _norm_proj(xf, norm1_g[l].reshape(1, d), sc1, sh1, w_main, w_small, dt_bias_pad, seq, l)

        wg_pad = jnp.concatenate([gla_w_gk2[l], jnp.zeros((LANES - GLA_RANK, GLA_DK), F32)], axis=0)
        o_a = _gla(proj, small, wg_pad, gla_b_gk[l].reshape(1, GLA_DK),
                   gla_norm_g[l].reshape(1, GLA_HV), bsz, seq)
        o_b = _ssd(proj, small, conv_w[l], conv_b[l].reshape(1, -1),
                   jnp.repeat(a_log[l], SSM_HEADDIM).reshape(1, SSM_DINNER),
                   jnp.repeat(d_skip[l], SSM_HEADDIM).reshape(1, SSM_DINNER),
                   ssm_norm_g[l].reshape(1, SSM_DINNER), bsz, seq)

        w_r = jnp.concatenate([router_group_w[l], router_expert_w[l],
                               jnp.zeros((d, LANES - MOE_GROUPS - MOE_EXPERTS), F32)], axis=1)
        b_r = jnp.concatenate([router_group_b[l], router_expert_b[l],
                               jnp.zeros((LANES - MOE_GROUPS - MOE_EXPERTS,), F32)]).reshape(1, LANES)
        x1, h2, ri, rw, cnt = _merge(o_a, o_b, proj, xf, gt1, w_oa[l].astype(BF16), w_ob[l].astype(BF16),
                                     w_out[l].astype(BF16), norm2_g[l].reshape(1, d), sc2, sh2, w_r, b_r, seq)

        pos, ends, tile_expert, n_valid = _routing_tables(ri, cnt, n_tiles)
        xs = _dispatch(pos, ends, h2, n_rows)
        ys = _experts(tile_expert, n_valid, xs,
                      expert_w1.reshape(depth * MOE_EXPERTS, d, MOE_DFF),
                      expert_w3.reshape(depth * MOE_EXPERTS, d, MOE_DFF),
                      expert_w2.reshape(depth * MOE_EXPERTS, MOE_DFF, d), l)
        xf = _combine(pos, ys, x1, rw, gt2, final_norm_g.reshape(1, d), seq, final=(l == depth - 1))
    return xf.reshape(bsz, seq, d)
```

```python
import functools

import jax
import jax.numpy as jnp
from jax import lax
from jax.experimental import pallas as pl
from jax.experimental.pallas import tpu as pltpu

F32 = jnp.float32
BF16 = jnp.bfloat16

EPS = 1e-6
CHUNK = 64
D_MODEL = 1024
N_MOD = 6
GLA_HEADS = 4
GLA_HK = 128
GLA_HV = 256
GLA_DK = GLA_HEADS * GLA_HK
GLA_DV = GLA_HEADS * GLA_HV
GLA_RANK = 16
GLA_NORMALIZER = 16.0
SSM_DINNER = 2048
SSM_HEADDIM = 64
SSM_HEADS = SSM_DINNER // SSM_HEADDIM
SSM_GROUPS = 8
SSM_HPG = SSM_HEADS // SSM_GROUPS
SSM_GW = SSM_HPG * SSM_HEADDIM
SSM_DSTATE = 128
SSM_CONV = 4
MOE_GROUPS = 4
MOE_EPG = 8
MOE_EXPERTS = MOE_GROUPS * MOE_EPG
MOE_DFF = 256

OFF_Q = 0
OFF_K = OFF_Q + GLA_DK
OFF_V = OFF_K + GLA_DK
OFF_G = OFF_V + GLA_DV
OFF_Z = OFF_G + GLA_DV
OFF_XS = OFF_Z + SSM_DINNER
OFF_B = OFF_XS + SSM_DINNER
OFF_C = OFF_B + SSM_GROUPS * SSM_DSTATE
OFF_GA = OFF_C + SSM_GROUPS * SSM_DSTATE
OFF_GB = OFF_GA + D_MODEL
N_PROJ = OFF_GB + D_MODEL
LANES = 128
SUBLANES = 8
SM_DT = GLA_RANK
RL_E = MOE_GROUPS

V7X_VMEM_LIMIT = 56 * 1024 * 1024

TM_PROJ = 1024
NJ_PROJ = 4
TN_PROJ = N_PROJ // NJ_PROJ
TM_MERGE = 512
TM_EXPERT = 512
TM_COMBINE = 512
TM_POS = 2048
DISPATCH_BLK = 1024
MIX_BLK = 256
CPB = MIX_BLK // CHUNK
CONV_ROWS = 128
CONV_HALO = 16
CONV_WIN = CONV_HALO + CONV_ROWS


def _cparams(sem):
    return pltpu.CompilerParams(dimension_semantics=sem, vmem_limit_bytes=V7X_VMEM_LIMIT)


def _sig(x):
    return 0.5 * jnp.tanh(0.5 * x) + 0.5


def _softplus(x):
    return jnp.maximum(x, 0.0) + jnp.log(1.0 + jnp.exp(-jnp.abs(x)))


def _log_sigmoid(x):
    return jnp.minimum(x, 0.0) - jnp.log(1.0 + jnp.exp(-jnp.abs(x)))


def _split(a):
    hi = a.astype(BF16)
    lo = (a - hi.astype(F32)).astype(BF16)
    return hi, lo


def _mm(a, b):
    return jnp.dot(a, b, preferred_element_type=F32)


def _dot3(a, b):
    ah, al = _split(a)
    bh, bl = _split(b)
    return _mm(ah, bh) + _mm(al, bh) + _mm(ah, bl)


def _dot2_lhs(a, b_bf16):
    ah, al = _split(a)
    return _mm(ah, b_bf16) + _mm(al, b_bf16)


def _chunk_scan_matrix(n):
    r = lax.broadcasted_iota(jnp.int32, (n, n), 0)
    c = lax.broadcasted_iota(jnp.int32, (n, n), 1)
    return jnp.where((r // CHUNK == c // CHUNK) & (r >= c), 1.0, 0.0).astype(BF16)


def _chunk_scan(mat, x):
    n, w = x.shape
    hi, lo = _split(x)
    out = _mm(mat, jnp.concatenate([hi, lo], axis=1))
    cum = out[:, :w] + out[:, w:]
    last_rows = [cum[c * CHUNK + CHUNK - 1:(c + 1) * CHUNK, :] for c in range(n // CHUNK)]
    tot = jnp.concatenate([jnp.broadcast_to(row, (CHUNK, w)) for row in last_rows], axis=0)
    return cum, tot


def _rms(x):
    return x * lax.rsqrt(jnp.mean(x * x, axis=-1, keepdims=True) + EPS)


def _pack_halves(x):
    w = x.shape[1] // 2
    lo = lax.bitcast_convert_type(x[:, :w].astype(BF16).astype(F32), jnp.uint32)
    hi = lax.bitcast_convert_type(x[:, w:].astype(BF16).astype(F32), jnp.uint32)
    return (lo >> 16) | (hi & jnp.uint32(0xFFFF0000))


def _unpack_halves(p):
    lo = lax.bitcast_convert_type(p << 16, F32)
    hi = lax.bitcast_convert_type(p & jnp.uint32(0xFFFF0000), F32)
    return lo, hi


def _skewed_loop(n, stages):
    depth = len(stages)

    def step(i, lo, hi):
        for k in reversed(range(lo, hi)):
            stages[k](i - k)

    for i in range(depth - 1):
        step(i, 0, i + 1)

    def body(i, carry):
        step(i, 0, depth)
        return carry

    lax.fori_loop(depth - 1, n, body, 0)
    for i in range(n, n + depth - 1):
        step(i, i - n + 1, depth)


def _item_rows(i):
    start = i * MIX_BLK
    return start if isinstance(start, int) else pl.multiple_of(start, MIX_BLK)


def _mod_kernel(c_ref, w_ref, b_ref, o_ref):
    c = c_ref[...]
    o_ref[0] = _dot3(c * _sig(c), w_ref[0]) + b_ref[0]


def _modulation(c, ada_w, ada_b):
    depth, d, n = ada_w.shape
    bsz = c.shape[0]
    tn = 1536
    return pl.pallas_call(
        _mod_kernel,
        grid=(depth, n // tn),
        in_specs=[
            pl.BlockSpec((bsz, d), lambda l, j: (0, 0)),
            pl.BlockSpec((1, d, tn), lambda l, j: (l, 0, j)),
            pl.BlockSpec((1, 1, tn), lambda l, j: (l, 0, j)),
        ],
        out_specs=pl.BlockSpec((1, bsz, tn), lambda l, j: (l, 0, j)),
        out_shape=jax.ShapeDtypeStruct((depth, bsz, n), F32),
        compiler_params=_cparams(("arbitrary", "arbitrary")),
        name="adaln_mod",
    )(c, ada_w, ada_b.reshape(depth, 1, n))


W_UNIT = 512
W_GK_UNIT = (OFF_G + GLA_DV) // W_UNIT
W_DT_UNIT = OFF_GA // W_UNIT


def _w_in_kernel(a_ref, b_ref, wm_ref, ws_ref, win_s):
    u = pl.program_id(1)
    win_s[:W_UNIT, :] = a_ref[0]
    win_s[W_UNIT:, :] = b_ref[0]
    cut = jnp.where(u < W_GK_UNIT, 0, jnp.where(u < W_DT_UNIT, GLA_RANK, GLA_RANK + SSM_HEADS))
    wm_ref[0] = win_s[pl.ds(pl.multiple_of(cut, GLA_RANK), W_UNIT), :].astype(BF16).T

    def side_columns():
        head = win_s[:LANES, :].T
        return head, lax.broadcasted_iota(jnp.int32, head.shape, 1)

    @pl.when(u == W_GK_UNIT)
    def _():
        head, lane = side_columns()
        ws_ref[0] = jnp.where(lane < GLA_RANK, head, 0.0)

    @pl.when(u == W_DT_UNIT)
    def _():
        head, lane = side_columns()
        ws_ref[0] = jnp.where((lane >= SM_DT) & (lane < SM_DT + SSM_HEADS), head, ws_ref[0])


def _prep_w_in(w_in_t):
    depth, _, d = w_in_t.shape
    return pl.pallas_call(
        _w_in_kernel,
        grid=(depth, N_PROJ // W_UNIT),
        in_specs=[
            pl.BlockSpec((1, W_UNIT, d), lambda l, u: (l, u, 0)),
            pl.BlockSpec((1, LANES, d), lambda l, u: (l, (W_UNIT // LANES) * (u + 1), 0)),
        ],
        out_specs=[
            pl.BlockSpec((1, d, W_UNIT), lambda l, u: (l, 0, u)),
            pl.BlockSpec((1, d, LANES), lambda l, u: (l, 0, 0)),
        ],
        out_shape=[
            jax.ShapeDtypeStruct((depth, d, N_PROJ), BF16),
            jax.ShapeDtypeStruct((depth, d, LANES), F32),
        ],
        scratch_shapes=[pltpu.VMEM((W_UNIT + LANES, d), F32)],
        compiler_params=_cparams(("arbitrary", "arbitrary")),
        name="w_in_layout",
    )(w_in_t, w_in_t)


def _norm_proj_kernel(x0_ref, xn_ref, g_ref, sc0_ref, sh0_ref, scn_ref, shn_ref, w_ref, ws_ref, dtb_ref,
                      o_ref, os_ref, h_s, side_s):
    i = pl.program_id(0)
    j = pl.program_id(1)
    cur = i % 2
    rows_per_step = TM_PROJ // NJ_PROJ

    def norm_rows(x_ref, sc_ref, sh_ref, rows, slot):
        h = _rms(x_ref[rows, :]) * g_ref[...] * (1.0 + sc_ref[0]) + sh_ref[0]
        h_s[slot, rows, :] = h.astype(BF16)
        side = _dot3(h, ws_ref[...])
        lane = lax.broadcasted_iota(jnp.int32, side.shape, 1)
        is_dt = (lane >= SM_DT) & (lane < SM_DT + SSM_HEADS)
        side_s[slot, rows, :] = jnp.where(is_dt, _softplus(side + dtb_ref[...]), side)

    @pl.when((i == 0) & (j == 0))
    def _():
        for r in range(NJ_PROJ):
            norm_rows(x0_ref, sc0_ref, sh0_ref, pl.ds(r * rows_per_step, rows_per_step), 0)

    norm_rows(xn_ref, scn_ref, shn_ref,
              pl.ds(pl.multiple_of(j * rows_per_step, rows_per_step), rows_per_step), 1 - cur)

    @pl.when(j == 0)
    def _():
        os_ref[...] = side_s[cur]

    o_ref[...] = _mm(h_s[cur], w_ref[...]).astype(BF16)


def _norm_proj(x, g, sc, sh, w_main, w_small, dt_bias_pad, seq, layer):
    t, d = x.shape
    tpb = seq // TM_PROJ
    last = t // TM_PROJ - 1
    nxt = lambda i: jnp.minimum(i + 1, last)
    return pl.pallas_call(
        _norm_proj_kernel,
        grid=(t // TM_PROJ, NJ_PROJ),
        in_specs=[
            pl.BlockSpec((TM_PROJ, d), lambda i, j: (0, 0)),
            pl.BlockSpec((TM_PROJ, d), lambda i, j: (nxt(i), 0)),
            pl.BlockSpec((1, d), lambda i, j: (0, 0)),
            pl.BlockSpec((1, 1, d), lambda i, j: (0, 0, 0)),
            pl.BlockSpec((1, 1, d), lambda i, j: (0, 0, 0)),
            pl.BlockSpec((1, 1, d), lambda i, j: (nxt(i) // tpb, 0, 0)),
            pl.BlockSpec((1, 1, d), lambda i, j: (nxt(i) // tpb, 0, 0)),
            pl.BlockSpec((None, d, TN_PROJ), lambda i, j: (layer, 0, j)),
            pl.BlockSpec((None, d, LANES), lambda i, j: (layer, 0, 0)),
            pl.BlockSpec((1, LANES), lambda i, j: (0, 0)),
        ],
        out_specs=[
            pl.BlockSpec((TM_PROJ, TN_PROJ), lambda i, j: (i, j)),
            pl.BlockSpec((TM_PROJ, LANES), lambda i, j: (i, 0)),
        ],
        out_shape=[
            jax.ShapeDtypeStruct((t, N_PROJ), BF16),
            jax.ShapeDtypeStruct((t, LANES), F32),
        ],
        scratch_shapes=[pltpu.VMEM((2, TM_PROJ, d), BF16), pltpu.VMEM((2, TM_PROJ, LANES), F32)],
        compiler_params=_cparams(("arbitrary", "arbitrary")),
        name="norm_in_proj",
    )(x, x, g, sc, sh, sc, sh, w_main, w_small, dt_bias_pad)


def _gla_kernel(q_ref, k_ref, v_ref, g_ref, sm_ref, wg_ref, bg_ref, ng_ref, o_ref,
                gk_s, kdec_s, edec_s, upd_s, st_s, oraw_s):
    seq = q_ref.shape[0]
    scan = _chunk_scan_matrix(MIX_BLK)
    wg_hi, wg_lo = _split(wg_ref[...])
    bg = bg_ref[...]
    ng = ng_ref[...]
    scale = GLA_HK ** -0.5
    st_s[...] = jnp.zeros_like(st_s)

    def stage_gate(i):
        rows = pl.ds(_item_rows(i), MIX_BLK)
        sm_hi, sm_lo = _split(sm_ref[rows, :])
        pre = _mm(sm_hi, wg_hi) + _mm(sm_lo, wg_hi) + _mm(sm_hi, wg_lo) + bg
        gk_s[rows, :] = _log_sigmoid(pre) * (1.0 / GLA_NORMALIZER)

    def stage_decay(i):
        rows = pl.ds(_item_rows(i), MIX_BLK)
        cum, tot = _chunk_scan(scan, gk_s[rows, :])
        kdec_s[rows, :] = (k_ref[rows, :].astype(F32) * jnp.exp(tot - cum)).astype(BF16)
        edec_s[rows, :] = jnp.exp(tot)

    def stage_update(i):
        for c in range(CPB):
            rows = pl.ds(_item_rows(i) + c * CHUNK, CHUNK)
            upd_s[i * CPB + c] = lax.dot_general(
                v_ref[rows, :], kdec_s[rows, :], (((0,), (0,)), ((), ())), preferred_element_type=F32)

    def stage_state(i):
        st = st_s[...]
        for c in range(CPB):
            r0 = _item_rows(i) + c * CHUNK
            st = edec_s[pl.ds(r0, 1), :] * st + upd_s[i * CPB + c]
            oraw_s[pl.ds(r0, CHUNK), :] = lax.dot_general(
                q_ref[pl.ds(r0, CHUNK), :], st.astype(BF16), (((1,), (1,)), ((), ())),
                preferred_element_type=F32)
        st_s[...] = st

    def stage_out(i):
        rows = pl.ds(_item_rows(i), MIX_BLK)
        o = _rms(oraw_s[rows, :] * scale) * ng
        g = g_ref[rows, :].astype(F32)
        o_ref[rows, :] = (o * (g * _sig(g))).astype(BF16)

    _skewed_loop(seq // MIX_BLK, [stage_gate, stage_decay, stage_update, stage_state, stage_out])


def _gla(proj, small, wg_pad, b_gk, norm_g, bsz, seq):
    t = proj.shape[0]
    kq, kk = OFF_Q // GLA_HK, OFF_K // GLA_HK
    kv, kg = OFF_V // GLA_HV, OFF_G // GLA_HV
    return pl.pallas_call(
        _gla_kernel,
        grid=(bsz, GLA_HEADS),
        in_specs=[
            pl.BlockSpec((seq, GLA_HK), lambda b, h: (b, kq + h)),
            pl.BlockSpec((seq, GLA_HK), lambda b, h: (b, kk + h)),
            pl.BlockSpec((seq, GLA_HV), lambda b, h: (b, kv + h)),
            pl.BlockSpec((seq, GLA_HV), lambda b, h: (b, kg + h)),
            pl.BlockSpec((seq, LANES), lambda b, h: (b, 0)),
            pl.BlockSpec((LANES, GLA_HK), lambda b, h: (0, h)),
            pl.BlockSpec((1, GLA_HK), lambda b, h: (0, h)),
            pl.BlockSpec((1, GLA_HV), lambda b, h: (0, 0)),
        ],
        out_specs=pl.BlockSpec((seq, GLA_HV), lambda b, h: (b, h)),
        out_shape=jax.ShapeDtypeStruct((t, GLA_DV), BF16),
        scratch_shapes=[
            pltpu.VMEM((seq, GLA_HK), F32),
            pltpu.VMEM((seq, GLA_HK), BF16),
            pltpu.VMEM((seq, GLA_HK), F32),
            pltpu.VMEM((seq // CHUNK, GLA_HV, GLA_HK), F32),
            pltpu.VMEM((GLA_HV, GLA_HK), F32),
            pltpu.VMEM((seq, GLA_HV), F32),
        ],
        compiler_params=_cparams(("arbitrary", "arbitrary")),
        name="gla_mixer",
    )(proj, proj, proj, proj, small, wg_pad, b_gk, norm_g)


def _conv_shift_matrix():
    t = lax.broadcasted_iota(jnp.int32, (CONV_ROWS, SSM_CONV * CONV_WIN), 0)
    c = lax.broadcasted_iota(jnp.int32, (CONV_ROWS, SSM_CONV * CONV_WIN), 1)
    return jnp.where(c % CONV_WIN == t + CONV_HALO - c // CONV_WIN, 1.0, 0.0).astype(BF16)


def _conv_window(ref, r0):
    if isinstance(r0, int) and r0 == 0:
        return jnp.concatenate([jnp.zeros((CONV_HALO, ref.shape[1]), BF16), ref[0:CONV_ROWS, :]], axis=0)
    start = r0 - CONV_HALO
    return ref[pl.ds(start if isinstance(start, int) else pl.multiple_of(start, CONV_HALO), CONV_WIN), :]


def _ssd_kernel(z_ref, xs_ref, b_ref, c_ref, sm_ref, cwx_ref, cwb_ref, cwc_ref, cbx_ref, cbb_ref,
                cbc_ref, alog_ref, dsk_ref, ng_ref, o_ref,
                xs_s, b_s, c_s, dt_s, a_s, ea_s, edec_s, xdt_s, decx_s, m_s, upd_s, st_s, y_s):
    seq = z_ref.shape[0]
    grp = pl.program_id(1)
    er = lax.broadcasted_iota(jnp.int32, (LANES, SSM_GW), 0)
    el = lax.broadcasted_iota(jnp.int32, (LANES, SSM_GW), 1)
    expand = jnp.where(er == SM_DT + SSM_HPG * grp + el // SSM_HEADDIM, 1.0, 0.0).astype(BF16)
    scan = _chunk_scan_matrix(MIX_BLK)
    a_row = -jnp.exp(alog_ref[...])
    dsk = dsk_ref[...]
    ng = ng_ref[...]
    dr = lax.broadcasted_iota(jnp.int32, (CHUNK, SSM_GW), 0)
    dl = lax.broadcasted_iota(jnp.int32, (CHUNK, SSM_GW), 1)
    diag = jnp.where(dr == dl % CHUNK, 1.0, 0.0)
    br = lax.broadcasted_iota(jnp.int32, (SSM_GW, SSM_GW), 0)
    bl = lax.broadcasted_iota(jnp.int32, (SSM_GW, SSM_GW), 1)
    blockmask = jnp.where(br // SSM_HEADDIM == bl // SSM_HEADDIM, 1.0, 0.0).astype(BF16)
    st_s[...] = jnp.zeros_like(st_s)
    shift = _conv_shift_matrix()
    taps = jnp.concatenate([cwx_ref[...], cwb_ref[...], cwc_ref[...]], axis=1)
    conv_bias = jnp.concatenate([cbx_ref[...], cbb_ref[...], cbc_ref[...]], axis=1)

    def stage_conv(i):
        rows = pl.ds(_item_rows(i), MIX_BLK)
        dt_s[rows, :] = _dot2_lhs(sm_ref[rows, :], expand)
        for sub in range(MIX_BLK // CONV_ROWS):
            r0 = _item_rows(i) + sub * CONV_ROWS
            win = jnp.concatenate(
                [_conv_window(xs_ref, r0), _conv_window(b_ref, r0), _conv_window(c_ref, r0)], axis=1)
            scaled = jnp.concatenate(
                [win * taps[SSM_CONV - 1 - k:SSM_CONV - k, :].astype(BF16) for k in range(SSM_CONV)], axis=0)
            acc = _mm(shift, scaled) + conv_bias
            act = acc * _sig(acc)
            out_rows = pl.ds(r0, CONV_ROWS)
            xs_s[out_rows, :] = act[:, :SSM_GW]
            b_s[out_rows, :] = act[:, SSM_GW:SSM_GW + SSM_DSTATE].astype(BF16)
            c_s[out_rows, :] = act[:, SSM_GW + SSM_DSTATE:].astype(BF16)

    def stage_decay(i):
        rows = pl.ds(_item_rows(i), MIX_BLK)
        dt = dt_s[rows, :]
        acum, atot = _chunk_scan(scan, dt * a_row)
        a_s[rows, :] = acum
        ea_s[rows, :] = jnp.exp(acum)
        edec_s[rows, :] = jnp.exp(atot)
        xdt = xs_s[rows, :] * dt
        xdt_s[rows, :] = xdt.astype(BF16)
        decx_s[rows, :] = (xdt * jnp.exp(atot - acum)).astype(BF16)

    def stage_intra(i):
        for c in range(CPB):
            rows = pl.ds(_item_rows(i) + c * CHUNK, CHUNK)
            acum = a_s[rows, :]
            acum_s = jnp.sum(acum * diag, axis=0, keepdims=True)
            lmask = jnp.exp(-jnp.abs(acum - acum_s))
            bm = b_s[rows, :]
            cb = lax.dot_general(c_s[rows, :], jnp.concatenate([bm] * SSM_HPG, axis=0),
                                 (((1,), (1,)), ((), ())), preferred_element_type=F32)
            m_s[rows, :] = (cb * lmask).astype(BF16)
            upd_s[i * CPB + c] = lax.dot_general(bm, decx_s[rows, :], (((0,), (0,)), ((), ())),
                                                 preferred_element_type=F32)

    def stage_state(i):
        st = st_s[...]
        for c in range(CPB):
            r0 = _item_rows(i) + c * CHUNK
            rows = pl.ds(r0, CHUNK)
            xdt_bd = jnp.concatenate([xdt_s[rows, :]] * SSM_HPG, axis=0) * blockmask
            y = _mm(m_s[rows, :], xdt_bd) + _mm(c_s[rows, :], st.astype(BF16)) * ea_s[rows, :]
            st = edec_s[pl.ds(r0, 1), :] * st + upd_s[i * CPB + c]
            y = y + xs_s[rows, :] * dsk
            z = z_ref[rows, :].astype(F32)
            y_s[rows, :] = y * (z * _sig(z))
        st_s[...] = st

    def stage_out(i):
        rows = pl.ds(_item_rows(i), MIX_BLK)
        o_ref[rows, :] = (_rms(y_s[rows, :]) * ng).astype(BF16)

    _skewed_loop(seq // MIX_BLK, [stage_conv, stage_decay, stage_intra, stage_state, stage_out])


def _ssd(proj, small, conv_w, conv_b, a_log_e, d_skip_e, norm_g, bsz, seq):
    t = proj.shape[0]
    kz, kx = OFF_Z // SSM_GW, OFF_XS // SSM_GW
    kb, kc = OFF_B // SSM_DSTATE, OFF_C // SSM_DSTATE
    nb = SSM_DINNER // SSM_DSTATE
    ncg = SSM_GROUPS
    wide_f32 = pltpu.VMEM((seq, SSM_GW), F32)
    wide_bf16 = pltpu.VMEM((seq, SSM_GW), BF16)
    return pl.pallas_call(
        _ssd_kernel,
        grid=(bsz, SSM_GROUPS),
        in_specs=[
            pl.BlockSpec((seq, SSM_GW), lambda b, g: (b, kz + g)),
            pl.BlockSpec((seq, SSM_GW), lambda b, g: (b, kx + g)),
            pl.BlockSpec((seq, SSM_DSTATE), lambda b, g: (b, kb + g)),
            pl.BlockSpec((seq, SSM_DSTATE), lambda b, g: (b, kc + g)),
            pl.BlockSpec((seq, LANES), lambda b, g: (b, 0)),
            pl.BlockSpec((SSM_CONV, SSM_GW), lambda b, g: (0, g)),
            pl.BlockSpec((SSM_CONV, SSM_DSTATE), lambda b, g: (0, nb + g)),
            pl.BlockSpec((SSM_CONV, SSM_DSTATE), lambda b, g: (0, nb + ncg + g)),
            pl.BlockSpec((1, SSM_GW), lambda b, g: (0, g)),
            pl.BlockSpec((1, SSM_DSTATE), lambda b, g: (0, nb + g)),
            pl.BlockSpec((1, SSM_DSTATE), lambda b, g: (0, nb + ncg + g)),
            pl.BlockSpec((1, SSM_GW), lambda b, g: (0, g)),
            pl.BlockSpec((1, SSM_GW), lambda b, g: (0, g)),
            pl.BlockSpec((1, SSM_GW), lambda b, g: (0, g)),
        ],
        out_specs=pl.BlockSpec((seq, SSM_GW), lambda b, g: (b, g)),
        out_shape=jax.ShapeDtypeStruct((t, SSM_DINNER), BF16),
        scratch_shapes=[
            wide_f32,
            pltpu.VMEM((seq, SSM_DSTATE), BF16),
            pltpu.VMEM((seq, SSM_DSTATE), BF16),
            wide_f32,
            wide_f32,
            wide_f32,
            wide_f32,
            wide_bf16,
            wide_bf16,
            wide_bf16,
            pltpu.VMEM((seq // CHUNK, SSM_DSTATE, SSM_GW), F32),
            pltpu.VMEM((SSM_DSTATE, SSM_GW), F32),
            wide_f32,
        ],
        compiler_params=_cparams(("arbitrary", "arbitrary")),
        name="ssd_mixer",
    )(proj, proj, proj, proj, small, conv_w, conv_w, conv_w, conv_b, conv_b, conv_b,
      a_log_e, d_skip_e, norm_g)


def _lane_min(x):
    return jnp.min(x, axis=-1, keepdims=True)


def _lane_max(x):
    return jnp.max(x, axis=-1, keepdims=True)


def _merge_kernel(oa_ref, ob_ref, ga_ref, gb_ref, x_ref, gt_ref, woa_ref, wob_ref, wout_ref,
                  g2_ref, sc_ref, sh_ref, wr_ref, br_ref,
                  x1_ref, h2_ref, ri_ref, rw_ref, cnt_ref, cnt_s, h_s):
    i = pl.program_id(0)
    tm = x_ref.shape[0]

    @pl.when(i == 0)
    def _():
        cnt_s[...] = jnp.zeros_like(cnt_s)
        h_s[...] = jnp.zeros_like(h_s)

    logit = _dot3(h_s[...], wr_ref[...]) + br_ref[...]

    ya = _mm(oa_ref[...], woa_ref[...])
    yb = _mm(ob_ref[...], wob_ref[...])

    lane = lax.broadcasted_iota(jnp.int32, (tm, LANES), 1)
    lane_f = lane.astype(F32)
    neg = jnp.float32(-jnp.inf)
    big = jnp.float32(LANES)
    is_g = lane < MOE_GROUPS
    lg = jnp.where(is_g, logit, neg)
    gmax = _lane_max(lg)
    gsum = jnp.sum(jnp.where(is_g, jnp.exp(lg - gmax), 0.0), axis=-1, keepdims=True)
    gidx = _lane_min(jnp.where(lg == gmax, lane_f, big))
    lo = RL_E + MOE_EPG * gidx
    in_grp = (lane_f >= lo) & (lane_f < lo + MOE_EPG)
    le = jnp.where(in_grp, logit, neg)
    emax = _lane_max(le)
    l0 = _lane_min(jnp.where(le == emax, lane_f, big))
    le2 = jnp.where(lane_f == l0, neg, le)
    emax2 = _lane_max(le2)
    l1 = _lane_min(jnp.where(le2 == emax2, lane_f, big))
    ratio = jnp.exp(emax2 - emax)
    w0 = 1.0 / (gsum * (1.0 + ratio))
    w1 = w0 * ratio

    counted = jnp.where(i > 0, 1.0, 0.0)
    oh0 = lane_f == l0
    oh1 = lane_f == l1
    oh = jnp.where(oh0 | oh1, counted, 0.0)
    tr = lax.broadcasted_iota(jnp.int32, (tm, tm), 0)
    tc = lax.broadcasted_iota(jnp.int32, (tm, tm), 1)
    before = _mm(jnp.where(tr > tc, 1.0, 0.0).astype(BF16), oh.astype(BF16)) + cnt_s[0:1, :]
    r0 = jnp.sum(jnp.where(oh0, before, 0.0), axis=-1, keepdims=True)
    r1 = jnp.sum(jnp.where(oh1, before, 0.0), axis=-1, keepdims=True)
    total = cnt_s[0:1, :] + jnp.sum(oh, axis=0, keepdims=True)
    cnt_s[...] = jnp.broadcast_to(total, cnt_s.shape)
    cnt_ref[...] = jnp.broadcast_to(total, cnt_ref.shape)

    ri = jnp.where(lane == 0, l0 - RL_E,
                   jnp.where(lane == 1, l1 - RL_E, jnp.where(lane == 2, r0, jnp.where(lane == 3, r1, 0.0))))
    ri_ref[...] = ri.astype(jnp.int32)
    rw_ref[...] = jnp.where(lane == 0, w0, jnp.where(lane == 1, w1, 0.0))

    merged = _sig(ga_ref[...].astype(F32)) * ya + _sig(gb_ref[...].astype(F32)) * yb
    x1 = x_ref[...] + gt_ref[0] * _mm(merged.astype(BF16), wout_ref[...])
    x1_ref[...] = x1
    h = _rms(x1) * g2_ref[...] * (1.0 + sc_ref[0]) + sh_ref[0]
    h2_ref[...] = _pack_halves(h)
    h_s[...] = h


def _merge(o_a, o_b, proj, x, gt1, w_oa, w_ob, w_out, g2, sc2, sh2, w_r, b_r, seq):
    t, d = x.shape
    tm = TM_MERGE
    tpb = seq // tm
    n = t // tm
    kga, kgb = OFF_GA // d, OFF_GB // d
    const = lambda i: (0, 0)
    cur = lambda i: jnp.minimum(i, n - 1)
    prev = lambda i: jnp.maximum(i - 1, 0)
    return pl.pallas_call(
        _merge_kernel,
        grid=(n + 1,),
        in_specs=[
            pl.BlockSpec((tm, GLA_DV), lambda i: (cur(i), 0)),
            pl.BlockSpec((tm, SSM_DINNER), lambda i: (cur(i), 0)),
            pl.BlockSpec((tm, d), lambda i: (cur(i), kga)),
            pl.BlockSpec((tm, d), lambda i: (cur(i), kgb)),
            pl.BlockSpec((tm, d), lambda i: (cur(i), 0)),
            pl.BlockSpec((1, 1, d), lambda i: (cur(i) // tpb, 0, 0)),
            pl.BlockSpec((GLA_DV, d), const),
            pl.BlockSpec((SSM_DINNER, d), const),
            pl.BlockSpec((d, d), const),
            pl.BlockSpec((1, d), const),
            pl.BlockSpec((1, 1, d), lambda i: (cur(i) // tpb, 0, 0)),
            pl.BlockSpec((1, 1, d), lambda i: (cur(i) // tpb, 0, 0)),
            pl.BlockSpec((d, LANES), const),
            pl.BlockSpec((1, LANES), const),
        ],
        out_specs=[
            pl.BlockSpec((tm, d), lambda i: (cur(i), 0)),
            pl.BlockSpec((tm, d // 2), lambda i: (cur(i), 0)),
            pl.BlockSpec((tm, LANES), lambda i: (prev(i), 0)),
            pl.BlockSpec((tm, LANES), lambda i: (prev(i), 0)),
            pl.BlockSpec((8, LANES), const),
        ],
        out_shape=[
            jax.ShapeDtypeStruct((t, d), F32),
            jax.ShapeDtypeStruct((t, d // 2), jnp.uint32),
            jax.ShapeDtypeStruct((t, LANES), jnp.int32),
            jax.ShapeDtypeStruct((t, LANES), F32),
            jax.ShapeDtypeStruct((8, LANES), F32),
        ],
        scratch_shapes=[pltpu.VMEM((8, LANES), F32), pltpu.VMEM((tm, d), F32)],
        compiler_params=_cparams(("arbitrary",)),
        name="merge_router",
    )(o_a, o_b, proj, proj, x, gt1, w_oa, w_ob, w_out, g2, sc2, sh2, w_r, b_r)


def _pos_kernel(ri_ref, st_ref, pos_ref):
    ri = ri_ref[...].astype(F32)
    lane = lax.broadcasted_iota(jnp.int32, ri.shape, 1)
    lane_f = lane.astype(F32)
    starts = st_ref[...]
    p0 = jnp.sum(jnp.where(lane_f == ri[:, 0:1], starts, 0.0), axis=-1, keepdims=True) + ri[:, 2:3]
    p1 = jnp.sum(jnp.where(lane_f == ri[:, 1:2], starts, 0.0), axis=-1, keepdims=True) + ri[:, 3:4]
    pos_ref[...] = jnp.where(lane == 0, p0, jnp.where(lane == 1, p1, 0.0)).astype(jnp.int32)


def _positions(ri, starts_lanes):
    t = ri.shape[0]
    return pl.pallas_call(
        _pos_kernel,
        grid=(t // TM_POS,),
        in_specs=[
            pl.BlockSpec((TM_POS, LANES), lambda i: (i, 0)),
            pl.BlockSpec((1, LANES), lambda i: (0, 0)),
        ],
        out_specs=pl.BlockSpec((TM_POS, LANES), lambda i: (i, 0)),
        out_shape=jax.ShapeDtypeStruct((t, LANES), jnp.int32),
        compiler_params=_cparams(("arbitrary",)),
        name="moe_positions",
    )(ri, starts_lanes)


def _dispatch_kernel(pos_ref, ends_ref, h_ref, xs_ref, zero_s, sem, zsem):
    n_tok = h_ref.shape[0] * SUBLANES

    @pl.when(pl.program_id(0) == 0)
    def _():
        zero_s[...] = jnp.zeros_like(zero_s)

        def last_tile(e):
            end = ends_ref[e]
            begin = jnp.where(e > 0, ends_ref[jnp.maximum(e - 1, 0)], 0)
            dst = xs_ref.at[pl.ds(pl.multiple_of(jnp.maximum(end - TM_EXPERT, 0), TM_EXPERT), TM_EXPERT)]
            return end > begin, pltpu.make_async_copy(zero_s, dst, zsem)

        def fill(e, carry):
            nonempty, cp = last_tile(e)

            @pl.when(nonempty)
            def _():
                cp.start()

            return carry

        def fill_done(e, carry):
            nonempty, cp = last_tile(e)

            @pl.when(nonempty)
            def _():
                cp.wait()

            return carry

        n_used = ends_ref[MOE_EXPERTS - 1] // TM_EXPERT

        def unused_tile(tile):
            dst = xs_ref.at[pl.ds(pl.multiple_of(tile * TM_EXPERT, TM_EXPERT), TM_EXPERT)]
            return pltpu.make_async_copy(zero_s, dst, zsem)

        def tail(tile, carry):
            unused_tile(tile).start()
            return carry

        def tail_done(tile, carry):
            unused_tile(tile).wait()
            return carry

        n_tiles = xs_ref.shape[0] // TM_EXPERT
        lax.fori_loop(0, MOE_EXPERTS, fill, 0)
        lax.fori_loop(n_used, n_tiles, tail, 0)
        lax.fori_loop(0, MOE_EXPERTS, fill_done, 0)
        lax.fori_loop(n_used, n_tiles, tail_done, 0)

    def issue(g, carry):
        for r in range(SUBLANES):
            tok = g * SUBLANES + r
            for k in range(2):
                pltpu.make_async_copy(h_ref.at[g, pl.ds(r, 1)], xs_ref.at[pl.ds(pos_ref[2 * tok + k], 1)],
                                      sem).start()
        return carry

    lax.fori_loop(0, n_tok // SUBLANES, issue, 0)

    def drain(tok, carry):
        for k in range(2):
            pltpu.make_async_copy(h_ref.at[0, pl.ds(0, 1)], xs_ref.at[pl.ds(0, 1)], sem).wait()
        return carry

    lax.fori_loop(0, n_tok, drain, 0, unroll=8)


def _dispatch(pos_flat, ends, h2, n_rows):
    t, d = h2.shape
    return pl.pallas_call(
        _dispatch_kernel,
        grid=(t // DISPATCH_BLK,),
        in_specs=[
            pl.BlockSpec((2 * DISPATCH_BLK,), lambda i: (i,), memory_space=pltpu.SMEM),
            pl.BlockSpec(memory_space=pltpu.SMEM),
            pl.BlockSpec((DISPATCH_BLK // SUBLANES, SUBLANES, d), lambda i: (i, 0, 0)),
        ],
        out_specs=pl.BlockSpec(memory_space=pl.ANY),
        out_shape=jax.ShapeDtypeStruct((n_rows, d), h2.dtype),
        scratch_shapes=[pltpu.VMEM((TM_EXPERT, d), h2.dtype), pltpu.SemaphoreType.DMA(()),
                        pltpu.SemaphoreType.DMA(())],
        compiler_params=_cparams(("arbitrary",)),
        name="moe_dispatch",
    )(pos_flat, ends, h2.reshape(t // SUBLANES, SUBLANES, d))


def _expert_kernel(te_ref, nv_ref, x_ref, w1_ref, w3_ref, w2_ref, o_ref, w13_s, w2_s):
    i = pl.program_id(0)
    valid = i < nv_ref[0]
    new_expert = (i == 0) | (te_ref[i] != te_ref[jnp.maximum(i - 1, 0)])

    @pl.when(valid & new_expert)
    def _():
        w13_s[:, :MOE_DFF] = w1_ref[0].astype(BF16)
        w13_s[:, MOE_DFF:] = w3_ref[0].astype(BF16)
        w2_s[...] = w2_ref[0].astype(BF16)

    @pl.when(valid)
    def _():
        half = w13_s.shape[0] // 2
        x_lo, x_hi = _unpack_halves(x_ref[...])
        ab = _mm(x_lo.astype(BF16), w13_s[:half, :]) + _mm(x_hi.astype(BF16), w13_s[half:, :])
        a = ab[:, :MOE_DFF]
        o_ref[...] = _pack_halves(_mm((a * _sig(a) * ab[:, MOE_DFF:]).astype(BF16), w2_s[...]))

    @pl.when(jnp.logical_not(valid))
    def _():
        o_ref[...] = jnp.zeros_like(o_ref)


def _experts(tile_expert, n_valid, xs, w1, w3, w2, layer):
    n_rows, dp = xs.shape
    d = 2 * dp
    tm = TM_EXPERT
    first = layer * MOE_EXPERTS
    grid_spec = pltpu.PrefetchScalarGridSpec(
        num_scalar_prefetch=2,
        grid=(n_rows // tm,),
        in_specs=[
            pl.BlockSpec((tm, dp), lambda i, te, nv: (jnp.minimum(i, nv[0] - 1), 0)),
            pl.BlockSpec((1, d, MOE_DFF), lambda i, te, nv: (first + te[i], 0, 0)),
            pl.BlockSpec((1, d, MOE_DFF), lambda i, te, nv: (first + te[i], 0, 0)),
            pl.BlockSpec((1, MOE_DFF, d), lambda i, te, nv: (first + te[i], 0, 0)),
        ],
        out_specs=pl.BlockSpec((tm, dp), lambda i, te, nv: (i, 0)),
        scratch_shapes=[pltpu.VMEM((d, 2 * MOE_DFF), BF16), pltpu.VMEM((MOE_DFF, d), BF16)],
    )
    return pl.pallas_call(
        _expert_kernel,
        grid_spec=grid_spec,
        out_shape=jax.ShapeDtypeStruct((n_rows, dp), jnp.uint32),
        compiler_params=_cparams(("arbitrary",)),
        name="moe_experts",
    )(tile_expert, n_valid, xs, w1, w3, w2)


def _combine_kernel(pos_ref, pos_next_ref, ys_ref, x_ref, rw_ref, gt_ref, fg_ref, o_ref, buf, sem, *, final):
    tm = x_ref.shape[0]
    i = pl.program_id(0)
    slot = i % 2

    def gather(p_ref, s):
        def issue(g, carry):
            for r in range(SUBLANES):
                tok = g * SUBLANES + r
                for k in range(2):
                    pltpu.make_async_copy(ys_ref.at[pl.ds(p_ref[2 * tok + k], 1)],
                                          buf.at[s, k, g, pl.ds(r, 1)], sem.at[s]).start()
            return carry

        lax.fori_loop(0, tm // SUBLANES, issue, 0)

    @pl.when(i == 0)
    def _():
        gather(pos_ref, 0)

    @pl.when(i + 1 < pl.num_programs(0))
    def _():
        gather(pos_next_ref, 1 - slot)

    def drain(tok, carry):
        for k in range(2):
            pltpu.make_async_copy(ys_ref.at[pl.ds(0, 1)], buf.at[slot, k, 0, pl.ds(0, 1)], sem.at[slot]).wait()
        return carry

    lax.fori_loop(0, tm, drain, 0, unroll=8)

    rw = rw_ref[...]
    w0, w1 = rw[:, 0:1], rw[:, 1:2]
    y0_lo, y0_hi = _unpack_halves(buf[slot, 0].reshape(tm, -1))
    y1_lo, y1_hi = _unpack_halves(buf[slot, 1].reshape(tm, -1))
    y = jnp.concatenate([w0 * y0_lo + w1 * y1_lo, w0 * y0_hi + w1 * y1_hi], axis=1)
    out = x_ref[...] + gt_ref[0] * y
    if final:
        out = _rms(out) * fg_ref[...]
    o_ref[...] = out


def _combine(pos_flat, ys, x1, rw, gt2, final_g, seq, final):
    t, d = x1.shape
    tm = TM_COMBINE
    tpb = seq // tm
    last = t // tm - 1
    return pl.pallas_call(
        functools.partial(_combine_kernel, final=final),
        grid=(t // tm,),
        in_specs=[
            pl.BlockSpec((2 * tm,), lambda i: (i,), memory_space=pltpu.SMEM),
            pl.BlockSpec((2 * tm,), lambda i: (jnp.minimum(i + 1, last),), memory_space=pltpu.SMEM),
            pl.BlockSpec(memory_space=pl.ANY),
            pl.BlockSpec((tm, d), lambda i: (i, 0)),
            pl.BlockSpec((tm, LANES), lambda i: (i, 0)),
            pl.BlockSpec((1, 1, d), lambda i: (i // tpb, 0, 0)),
            pl.BlockSpec((1, d), lambda i: (0, 0)),
        ],
        out_specs=pl.BlockSpec((tm, d), lambda i: (i, 0)),
        out_shape=jax.ShapeDtypeStruct((t, d), F32),
        scratch_shapes=[pltpu.VMEM((2, 2, tm // SUBLANES, SUBLANES, d // 2), jnp.uint32),
                        pltpu.SemaphoreType.DMA((2,))],
        compiler_params=_cparams(("arbitrary",)),
        name="moe_combine",
    )(pos_flat, pos_flat, ys, x1, rw, gt2, final_g)


def _routing_tables(ri, cnt, n_tiles):
    counts = cnt[0, RL_E:RL_E + MOE_EXPERTS].astype(jnp.int32)
    padded = ((counts + TM_EXPERT - 1) // TM_EXPERT) * TM_EXPERT
    ends = jnp.cumsum(padded)
    starts = ends - padded
    starts_lanes = jnp.concatenate(
        [starts.astype(F32), jnp.zeros((LANES - MOE_EXPERTS,), F32)]).reshape(1, LANES)
    pos = _positions(ri, starts_lanes)[:, :2].reshape(-1)
    n_valid = (ends[-1] // TM_EXPERT).astype(jnp.int32).reshape(1)
    tile_start = jnp.arange(n_tiles, dtype=jnp.int32) * TM_EXPERT
    tile_expert = jnp.minimum(
        jnp.sum((tile_start[:, None] >= ends[None, :]).astype(jnp.int32), axis=1), MOE_EXPERTS - 1)
    return pos, ends, tile_expert, n_valid


def kernel(x, c, ada_w, ada_b, norm1_g, w_in, gla_w_gk2, gla_b_gk, gla_norm_g, conv_w, conv_b,
           dt_bias, a_log, d_skip, ssm_norm_g, w_oa, w_ob, w_out, norm2_g, router_group_w,
           router_group_b, router_expert_w, router_expert_b, expert_w1, expert_w3, expert_w2,
           final_norm_g):
    bsz, seq, d = x.shape
    depth = ada_w.shape[0]
    t = bsz * seq
    n_tiles = (2 * t) // TM_EXPERT + MOE_EXPERTS
    n_rows = n_tiles * TM_EXPERT

    mod = _modulation(c, ada_w, ada_b)
    xf = x.reshape(t, d)
    w_main, w_small = _prep_w_in(jnp.swapaxes(w_in, 1, 2))
    for l in range(depth):
        sh1, sc1, gt1, sh2, sc2, gt2 = [mod[l, :, k * d:(k + 1) * d].reshape(bsz, 1, d) for k in range(N_MOD)]
        dt_bias_pad = jnp.concatenate([jnp.zeros((GLA_RANK,), F32), dt_bias[l],
                                       jnp.zeros((LANES - GLA_RANK - SSM_HEADS,), F32)]).reshape(1, LANES)
        proj, small = _norm_proj(xf, norm1_g[l].reshape(1, d), sc1, sh1, w_main, w_small, dt_bias_pad, seq, l)

        wg_pad = jnp.concatenate([gla_w_gk2[l], jnp.zeros((LANES - GLA_RANK, GLA_DK), F32)], axis=0)
        o_a = _gla(proj, small, wg_pad, gla_b_gk[l].reshape(1, GLA_DK),
                   gla_norm_g[l].reshape(1, GLA_HV), bsz, seq)
        o_b = _ssd(proj, small, conv_w[l], conv_b[l].reshape(1, -1),
                   jnp.repeat(a_log[l], SSM_HEADDIM).reshape(1, SSM_DINNER),
                   jnp.repeat(d_skip[l], SSM_HEADDIM).reshape(1, SSM_DINNER),
                   ssm_norm_g[l].reshape(1, SSM_DINNER), bsz, seq)

        w_r = jnp.concatenate([router_group_w[l], router_expert_w[l],
                               jnp.zeros((d, LANES - MOE_GROUPS - MOE_EXPERTS), F32)], axis=1)
        b_r = jnp.concatenate([router_group_b[l], router_expert_b[l],
                               jnp.zeros((LANES - MOE_GROUPS - MOE_EXPERTS,), F32)]).reshape(1, LANES)
        x1, h2, ri, rw, cnt = _merge(o_a, o_b, proj, xf, gt1, w_oa[l].astype(BF16), w_ob[l].astype(BF16),
                                     w_out[l].astype(BF16), norm2_g[l].reshape(1, d), sc2, sh2, w_r, b_r, seq)

        pos, ends, tile_expert, n_valid = _routing_tables(ri, cnt, n_tiles)
        xs = _dispatch(pos, ends, h2, n_rows)
        ys = _experts(tile_expert, n_valid, xs,
                      expert_w1.reshape(depth * MOE_EXPERTS, d, MOE_DFF),
                      expert_w3.reshape(depth * MOE_EXPERTS, d, MOE_DFF),
                      expert_w2.reshape(depth * MOE_EXPERTS, MOE_DFF, d), l)
        xf = _combine(pos, ys, x1, rw, gt2, final_norm_g.reshape(1, d), seq, final=(l == depth - 1))
    return xf.reshape(bsz, seq, d)
```
